```python
import math
import jax, jax.numpy as jnp
from jax import lax
import numpy as np

D_MODEL = 1024
BATCH = 4
SEQ = 8192
DEPTH = 1

HEAD_DIM = 64
SWA_Q_HEADS = 8
SWA_KV_HEADS = 2
SWA_GROUP = SWA_Q_HEADS // SWA_KV_HEADS
WINDOW = 128
DIFF_HEADS = 4
DIFF_V_DIM = 2 * HEAD_DIM
SWA_WIDTH = SWA_Q_HEADS * HEAD_DIM
DIFF_WIDTH = DIFF_HEADS * DIFF_V_DIM
MIX_WIDTH = SWA_WIDTH + DIFF_WIDTH
QA_COLS = SWA_Q_HEADS * HEAD_DIM
KA_COLS = SWA_KV_HEADS * HEAD_DIM
VA_COLS = SWA_KV_HEADS * HEAD_DIM
QB_COLS = DIFF_HEADS * 2 * HEAD_DIM
KB_COLS = DIFF_HEADS * 2 * HEAD_DIM
VB_COLS = DIFF_HEADS * DIFF_V_DIM
IN_COLS = QA_COLS + KA_COLS + VA_COLS + QB_COLS + KB_COLS + VB_COLS
IN_SPLITS = (QA_COLS, QA_COLS + KA_COLS, QA_COLS + KA_COLS + VA_COLS,
             QA_COLS + KA_COLS + VA_COLS + QB_COLS,
             QA_COLS + KA_COLS + VA_COLS + QB_COLS + KB_COLS)
Q_BLOCK = 128
D_FF = 4 * D_MODEL
CONV_WIDTH = 3
EPS = 1e-6

kernel_name = "hymba_swa_sink_diffattn_alibi_convffn"


def rms_norm(x, g):
    xf = x.astype(jnp.float32)
    xf = xf * lax.rsqrt(jnp.mean(xf * xf, axis=-1, keepdims=True) + EPS)
    return (xf * g.astype(jnp.float32)).astype(x.dtype)


def alibi_slopes(n):
    def pow2(m):
        start = 2.0 ** (-8.0 / m)
        return [start ** (i + 1) for i in range(m)]
    if math.log2(n).is_integer():
        s = pow2(n)
    else:
        c = 2 ** int(math.floor(math.log2(n)))
        s = pow2(c) + pow2(2 * c)[0::2][: n - c]
    return np.array(sorted(s, reverse=True), dtype=np.float32)


def swa_sink_attention(q, k, v, sinks, slopes):
    B, S = q.shape[0], q.shape[1]
    nb = S // WINDOW
    qb = q.reshape(B, nb, WINDOW, SWA_KV_HEADS, SWA_GROUP, HEAD_DIM)
    kb = k.reshape(B, nb, WINDOW, SWA_KV_HEADS, HEAD_DIM)
    vb = v.reshape(B, nb, WINDOW, SWA_KV_HEADS, HEAD_DIM)
    pad = ((0, 0), (1, 0), (0, 0), (0, 0), (0, 0))
    kk = jnp.concatenate([jnp.pad(kb, pad)[:, :-1], kb], axis=2)
    vv = jnp.concatenate([jnp.pad(vb, pad)[:, :-1], vb], axis=2)
    scores = jnp.einsum("bnqhgd,bnkhd->bnhgqk", qb, kk).astype(jnp.float32)
    scores = scores * (HEAD_DIM ** -0.5)
    qpos = jnp.arange(WINDOW)[:, None] + WINDOW
    kpos = jnp.arange(2 * WINDOW)[None, :]
    dist = qpos - kpos
    valid = (dist >= 0) & (dist < WINDOW)
    abs_k = jnp.arange(nb)[:, None, None] * WINDOW + kpos[None] - WINDOW
    valid = valid[None] & (abs_k >= 0)
    sl = slopes.reshape(SWA_KV_HEADS, SWA_GROUP)[:, :, None, None]
    scores = scores - sl * dist.astype(jnp.float32)[None, None]
    scores = jnp.where(valid[None, :, None, None], scores, -jnp.inf)
    sink = jnp.broadcast_to(
        sinks.astype(jnp.float32).reshape(1, 1, SWA_KV_HEADS, SWA_GROUP, 1, 1),
        scores.shape[:-1] + (1,))
    p = jax.nn.softmax(jnp.concatenate([scores, sink], axis=-1), axis=-1)[..., :-1]
    out = jnp.einsum("bnhgqk,bnkhd->bnqhgd", p.astype(v.dtype), vv)
    return out.reshape(B, S, SWA_Q_HEADS * HEAD_DIM)


def diff_attention(q, k, v, lam, slopes, subln_g, lambda_init):
    B, S = q.shape[0], q.shape[1]
    nb = S // Q_BLOCK
    qb = q.reshape(B, nb, Q_BLOCK, DIFF_HEADS, 2, HEAD_DIM).transpose(1, 0, 2, 3, 4, 5)
    kpos = jnp.arange(S)

    def block(args):
        q_blk, i = args
        s = jnp.einsum("bqhmd,bkhmd->bhmqk", q_blk, k).astype(jnp.float32)
        s = s * (HEAD_DIM ** -0.5)
        dist = (i * Q_BLOCK + jnp.arange(Q_BLOCK))[:, None] - kpos[None, :]
        s = s - slopes[None, :, None, None, None] * dist.astype(jnp.float32)
        s = jnp.where(dist >= 0, s, -jnp.inf)
        p = jax.nn.softmax(s, axis=-1)
        p_diff = p[:, :, 0] - lam * p[:, :, 1]
        return jnp.einsum("bhqk,bkhe->bqhe", p_diff.astype(v.dtype), v)

    out = lax.map(block, (qb, jnp.arange(nb)))
    out = out.transpose(1, 0, 2, 3, 4).reshape(B, S, DIFF_HEADS, DIFF_V_DIM)
    out = rms_norm(out, subln_g) * (1.0 - lambda_init)
    return out.reshape(B, S, DIFF_HEADS * DIFF_V_DIM)


def conv_ffn(h, w_up, conv_w, conv_b, w_down):
    u = jnp.einsum("bsd,df->bsf", h, w_up)
    up = jnp.pad(u, ((0, 0), (CONV_WIDTH - 1, 0), (0, 0)))
    S = u.shape[1]
    c = (conv_w[0] * up[:, 0:S] + conv_w[1] * up[:, 1:S + 1]
         + conv_w[2] * up[:, 2:S + 2] + conv_b)
    g, val = jnp.split(c, 2, axis=-1)
    return jnp.einsum("bsf,fd->bsd", jax.nn.gelu(g, approximate=True) * val, w_down)


def setup_inputs(seed: int = 0) -> dict:
    key = jax.random.key(seed)
    ks = jax.random.split(key, 18)
    f32 = jnp.float32

    def nrm(k, shape, scale):
        return jax.random.normal(k, shape, f32) * scale

    def gain(k, shape):
        return 1.0 + 0.05 * jax.random.normal(k, shape, f32)

    return {
        "x": nrm(ks[0], (BATCH, SEQ, D_MODEL), 1.0),
        "attn_pre_g": gain(ks[1], (DEPTH, D_MODEL)),
        "w_in": nrm(ks[2], (DEPTH, D_MODEL, IN_COLS), D_MODEL ** -0.5),
        "swa_sinks": nrm(ks[3], (DEPTH, SWA_Q_HEADS), 0.5),
        "swa_out_g": gain(ks[4], (DEPTH, SWA_WIDTH)),
        "diff_lq1": nrm(ks[5], (DEPTH, HEAD_DIM), 0.1),
        "diff_lk1": nrm(ks[6], (DEPTH, HEAD_DIM), 0.1),
        "diff_lq2": nrm(ks[7], (DEPTH, HEAD_DIM), 0.1),
        "diff_lk2": nrm(ks[8], (DEPTH, HEAD_DIM), 0.1),
        "diff_subln_g": gain(ks[9], (DEPTH, DIFF_V_DIM)),
        "w_out": nrm(ks[10], (DEPTH, MIX_WIDTH, D_MODEL), MIX_WIDTH ** -0.5),
        "attn_post_g": gain(ks[11], (DEPTH, D_MODEL)),
        "ffn_pre_g": gain(ks[12], (DEPTH, D_MODEL)),
        "w_up": nrm(ks[13], (DEPTH, D_MODEL, 2 * D_FF), D_MODEL ** -0.5),
        "conv_w": nrm(ks[14], (DEPTH, CONV_WIDTH, 2 * D_FF), CONV_WIDTH ** -0.5),
        "conv_b": nrm(ks[15], (DEPTH, 2 * D_FF), 0.02),
        "w_down": nrm(ks[16], (DEPTH, D_FF, D_MODEL), D_FF ** -0.5),
        "ffn_post_g": gain(ks[17], (DEPTH, D_MODEL)),
    }


def reference(x, attn_pre_g, w_in, swa_sinks, swa_out_g, diff_lq1, diff_lk1, diff_lq2,
              diff_lk2, diff_subln_g, w_out, attn_post_g, ffn_pre_g, w_up, conv_w, conv_b,
              w_down, ffn_post_g):
    B, S = x.shape[0], x.shape[1]
    slopes = jnp.asarray(alibi_slopes(SWA_Q_HEADS + DIFF_HEADS))
    swa_slopes = slopes[:SWA_Q_HEADS]
    diff_slopes = slopes[SWA_Q_HEADS:]
    for layer in range(DEPTH):
        lambda_init = 0.8 - 0.6 * math.exp(-0.3 * layer)
        h = rms_norm(x, attn_pre_g[layer])
        proj = jnp.einsum("bsd,de->bse", h, w_in[layer])
        q_a, k_a, v_a, q_b, k_b, v_b = jnp.split(proj, IN_SPLITS, axis=-1)
        y_a = swa_sink_attention(
            q_a.reshape(B, S, SWA_Q_HEADS, HEAD_DIM),
            k_a.reshape(B, S, SWA_KV_HEADS, HEAD_DIM),
            v_a.reshape(B, S, SWA_KV_HEADS, HEAD_DIM),
            swa_sinks[layer], swa_slopes)
        y_a = rms_norm(y_a, swa_out_g[layer])
        lam = (jnp.exp(jnp.sum(diff_lq1[layer].astype(jnp.float32) * diff_lk1[layer].astype(jnp.float32)))
               - jnp.exp(jnp.sum(diff_lq2[layer].astype(jnp.float32) * diff_lk2[layer].astype(jnp.float32)))
               + lambda_init)
        y_b = diff_attention(
            q_b.reshape(B, S, DIFF_HEADS, 2, HEAD_DIM),
            k_b.reshape(B, S, DIFF_HEADS, 2, HEAD_DIM),
            v_b.reshape(B, S, DIFF_HEADS, DIFF_V_DIM),
            lam, diff_slopes, diff_subln_g[layer], lambda_init)
        mix = jnp.concatenate([y_a, y_b], axis=-1)
        x = x + rms_norm(jnp.einsum("bse,ed->bsd", mix, w_out[layer]), attn_post_g[layer])
        h2 = rms_norm(x, ffn_pre_g[layer])
        f = conv_ffn(h2, w_up[layer], conv_w[layer], conv_b[layer], w_down[layer])
        x = x + rms_norm(f, ffn_post_g[layer])
    return x
```

```python
import functools
import math

import numpy as np
import jax
import jax.numpy as jnp
from jax import lax
from jax.experimental import pallas as pl
from jax.experimental.pallas import tpu as pltpu

F32 = jnp.float32
BF16 = jnp.bfloat16

EPS = 1e-6
HEAD_DIM = 64
SWA_Q_HEADS = 8
SWA_KV_HEADS = 2
SWA_GROUP = SWA_Q_HEADS // SWA_KV_HEADS
WINDOW = 128
DIFF_HEADS = 4
DIFF_V_DIM = 2 * HEAD_DIM
CONV_WIDTH = 3
NEG = -1e30

VMEM_LIMIT_BYTES = 56 * 1024 * 1024

PROJ_ROWS = 512
SWA_ROWS = 512
DIFF_TQ = 512
DIFF_TK = 512
FFN_ROWS = 512
FFN_CHUNK = 512
HALO = 16


def _alibi_slopes(n):
    def pow2(m):
        start = 2.0 ** (-8.0 / m)
        return [start ** (i + 1) for i in range(m)]
    if math.log2(n).is_integer():
        s = pow2(n)
    else:
        c = 2 ** int(math.floor(math.log2(n)))
        s = pow2(c) + pow2(2 * c)[0::2][: n - c]
    return np.array(sorted(s, reverse=True), dtype=np.float32)


def _rms(xf, g):
    return xf * lax.rsqrt(jnp.mean(xf * xf, axis=-1, keepdims=True) + EPS) * g


def _params(n_axes):
    return pltpu.CompilerParams(
        dimension_semantics=("arbitrary",) * n_axes,
        vmem_limit_bytes=VMEM_LIMIT_BYTES)


def _const_spec(shape):
    nd = len(shape)
    return pl.BlockSpec(shape, lambda *_: (0,) * nd, pipeline_mode=pl.Buffered(1))


def _proj_kernel(x_ref, g_ref, w_ref, wvt_ref,
                 qa_ref, ka_ref, va_ref, qb_ref, kb_ref, vbt_ref):
    h = _rms(x_ref[...], g_ref[...]).astype(BF16)
    y = jnp.dot(h, w_ref[...], preferred_element_type=F32)
    scale = HEAD_DIM ** -0.5
    o = 0
    for ref, sc in ((qa_ref, scale), (ka_ref, None), (va_ref, None),
                    (qb_ref, scale), (kb_ref, None)):
        w = ref.shape[-1]
        piece = y[:, o:o + w]
        if sc is not None:
            piece = piece * sc
        ref[...] = piece.astype(BF16)
        o += w
    vt = lax.dot_general(wvt_ref[...], h, (((1,), (1,)), ((), ())),
                         preferred_element_type=F32)
    vbt_ref[0, 0] = vt.astype(BF16)


def _proj(x2, g, w_main, w_vbt, batch, seq):
    t, d = x2.shape
    tm = PROJ_ROWS
    ns = seq // tm
    widths = (SWA_Q_HEADS * HEAD_DIM, SWA_KV_HEADS * HEAD_DIM, SWA_KV_HEADS * HEAD_DIM,
              DIFF_HEADS * 2 * HEAD_DIM, DIFF_HEADS * 2 * HEAD_DIM)
    vw = DIFF_HEADS * DIFF_V_DIM
    out_shape = [jax.ShapeDtypeStruct((t, w), BF16) for w in widths]
    out_shape.append(jax.ShapeDtypeStruct((batch, ns, vw, tm), BF16))
    out_specs = [pl.BlockSpec((tm, w), lambda i: (i, 0)) for w in widths]
    out_specs.append(pl.BlockSpec((1, 1, vw, tm), lambda i: (i // ns, i % ns, 0, 0)))
    return pl.pallas_call(
        _proj_kernel,
        grid=(t // tm,),
        in_specs=[pl.BlockSpec((tm, d), lambda i: (i, 0)),
                  _const_spec(g.shape), _const_spec(w_main.shape), _const_spec(w_vbt.shape)],
        out_specs=out_specs,
        out_shape=out_shape,
        compiler_params=_params(1),
        name="proj_in",
    )(x2, g, w_main, w_vbt)


def _swa_bias(slopes):
    qpos = np.arange(WINDOW)[:, None] + WINDOW
    kpos = np.arange(2 * WINDOW)[None, :]
    dist = (qpos - kpos).astype(np.float32)
    valid = (dist >= 0) & (dist < WINDOW)
    first_valid = valid & (kpos >= WINDOW)
    out = np.empty((2, SWA_KV_HEADS, SWA_GROUP * WINDOW, 2 * WINDOW), np.float32)
    for g in range(SWA_KV_HEADS):
        for j in range(SWA_GROUP):
            sl = slopes[g * SWA_GROUP + j]
            rows = slice(j * WINDOW, (j + 1) * WINDOW)
            out[0, g, rows] = np.where(valid, -sl * dist, NEG)
            out[1, g, rows] = np.where(first_valid, -sl * dist, NEG)
    return out


def _swa_kernel(q_ref, k_ref, kp_ref, v_ref, vp_ref, bias_ref, sink_ref, g_ref, o_ref):
    i = pl.program_id(1)
    nblk = q_ref.shape[1] // WINDOW
    for n in range(nblk):
        rows = slice(n * WINDOW, (n + 1) * WINDOW)
        if n == 0:
            kk_all = jnp.concatenate([kp_ref[0], k_ref[0, rows, :]], axis=0)
            vv_all = jnp.concatenate([vp_ref[0], v_ref[0, rows, :]], axis=0)
        else:
            kk_all = k_ref[0, (n - 1) * WINDOW:(n + 1) * WINDOW, :]
            vv_all = v_ref[0, (n - 1) * WINDOW:(n + 1) * WINDOW, :]
        q_all = q_ref[0, rows, :]
        outs = []
        for g in range(SWA_KV_HEADS):
            kk = kk_all[:, g * HEAD_DIM:(g + 1) * HEAD_DIM]
            vv = vv_all[:, g * HEAD_DIM:(g + 1) * HEAD_DIM]
            qs = jnp.concatenate(
                [q_all[:, (g * SWA_GROUP + j) * HEAD_DIM:(g * SWA_GROUP + j + 1) * HEAD_DIM]
                 for j in range(SWA_GROUP)], axis=0)
            s = lax.dot_general(qs, kk, (((1,), (1,)), ((), ())),
                                preferred_element_type=F32)
            if n == 0:
                bias = jnp.where(i == 0, bias_ref[1, g], bias_ref[0, g])
            else:
                bias = bias_ref[0, g]
            s = s + bias
            sink = sink_ref[g]
            m = jnp.maximum(jnp.max(s, axis=-1, keepdims=True), sink)
            p = jnp.exp(s - m)
            l = jnp.sum(p, axis=-1, keepdims=True) + jnp.exp(sink - m)
            o = jnp.dot(p.astype(BF16), vv, preferred_element_type=F32) / l
            outs.extend(o[j * WINDOW:(j + 1) * WINDOW] for j in range(SWA_GROUP))
        y = jnp.concatenate(outs, axis=-1)
        o_ref[0, rows, :] = _rms(y, g_ref[...]).astype(BF16)


def _swa(qa, ka, va, bias, sink_rows, g):
    b, s, qw = qa.shape
    kw = ka.shape[-1]
    tq = SWA_ROWS
    r = tq // WINDOW
    cur = lambda bi, i: (bi, i, 0)
    prev = lambda bi, i: (bi, jnp.maximum(i * r - 1, 0), 0)
    return pl.pallas_call(
        _swa_kernel,
        grid=(b, s // tq),
        in_specs=[pl.BlockSpec((1, tq, qw), cur),
                  pl.BlockSpec((1, tq, kw), cur), pl.BlockSpec((1, WINDOW, kw), prev),
                  pl.BlockSpec((1, tq, kw), cur), pl.BlockSpec((1, WINDOW, kw), prev),
                  _const_spec(bias.shape), _const_spec(sink_rows.shape), _const_spec(g.shape)],
        out_specs=pl.BlockSpec((1, tq, qw), cur),
        out_shape=jax.ShapeDtypeStruct((b, s, qw), BF16),
        compiler_params=_params(2),
        name="swa_attn",
    )(qa, ka, ka, va, va, bias, sink_rows, g)


def _diff_kernel(slopes_ref, q_ref, k_ref, vt_ref, lq1_ref, lk1_ref, lq2_ref, lk2_ref, g_ref,
                 o_ref, qp_ref, acc_ref, m_ref, l_ref, bias_ref, *, lambda_init):
    h = pl.program_id(1)
    i = pl.program_id(2)
    tq = q_ref.shape[1]
    tk = bias_ref.shape[0]
    slope = slopes_ref[h]

    @pl.when(i == 0)
    def _():
        kloc = lax.broadcasted_iota(jnp.int32, (tk, tq), 0).astype(F32)
        bias_ref[...] = slope * kloc

    q = q_ref[0]
    lane = lax.broadcasted_iota(jnp.int32, q.shape, 1)
    zero = jnp.zeros_like(q)
    qp_ref[0] = jnp.where(lane < HEAD_DIM, q, zero)
    qp_ref[1] = jnp.where(lane >= HEAD_DIM, q, zero)
    acc_ref[...] = jnp.zeros_like(acc_ref)
    l_ref[...] = jnp.zeros_like(l_ref)
    m_ref[...] = jnp.full_like(m_ref, NEG)

    def tile(j, diag_offset):
        kt = k_ref[0, pl.ds(pl.multiple_of(j * tk, tk), tk), :]
        vt = vt_ref[0, j]
        c = slope * (j * tk).astype(F32)
        for mp in range(2):
            s = lax.dot_general(kt, qp_ref[mp], (((1,), (1,)), ((), ())),
                                preferred_element_type=F32)
            s = s + bias_ref[...]
            if diag_offset is not None:
                krow = lax.broadcasted_iota(jnp.int32, (tk, tq), 0) + diag_offset
                qcol = lax.broadcasted_iota(jnp.int32, (tk, tq), 1)
                s = jnp.where(krow <= qcol, s, NEG)
            m_old = m_ref[mp]
            m_new = jnp.maximum(m_old, jnp.max(s, axis=0, keepdims=True) + c)
            alpha = jnp.exp(m_old - m_new)
            p = jnp.exp(s - (m_new - c))
            l_ref[mp] = alpha * l_ref[mp] + jnp.sum(p, axis=0, keepdims=True)
            acc_ref[mp] = alpha * acc_ref[mp] + jnp.dot(
                vt, p.astype(BF16), preferred_element_type=F32)
            m_ref[mp] = m_new

    ndiag = tq // tk
    n_full = i * ndiag

    def body(j, carry):
        tile(j, None)
        return carry

    lax.fori_loop(0, n_full, body, 0)
    for d in range(ndiag):
        tile(n_full + d, d * tk)

    lam = (jnp.exp(jnp.sum(lq1_ref[...] * lk1_ref[...], axis=-1, keepdims=True))
           - jnp.exp(jnp.sum(lq2_ref[...] * lk2_ref[...], axis=-1, keepdims=True))
           + lambda_init)
    y = acc_ref[0] * (1.0 / l_ref[0]) - lam * (acc_ref[1] * (1.0 / l_ref[1]))
    yt = y.T
    o_ref[0] = (_rms(yt, g_ref[...]) * (1.0 - lambda_init)).astype(BF16)


def _diff(qb, kb, vbt, slopes, lq1, lk1, lq2, lk2, g, lambda_init):
    b, s, _ = qb.shape
    tq, tk = DIFF_TQ, DIFF_TK
    w = DIFF_V_DIM
    ntk = vbt.shape[1]
    grid_spec = pltpu.PrefetchScalarGridSpec(
        num_scalar_prefetch=1,
        grid=(b, DIFF_HEADS, s // tq),
        in_specs=[pl.BlockSpec((1, tq, w), lambda bi, h, i, sl: (bi, i, h)),
                  pl.BlockSpec((1, s, w), lambda bi, h, i, sl: (bi, 0, h)),
                  pl.BlockSpec((1, ntk, w, tk), lambda bi, h, i, sl: (bi, 0, h, 0)),
                  _const_spec(lq1.shape), _const_spec(lk1.shape),
                  _const_spec(lq2.shape), _const_spec(lk2.shape), _const_spec(g.shape)],
        out_specs=pl.BlockSpec((1, tq, w), lambda bi, h, i, sl: (bi, i, h)),
        scratch_shapes=[pltpu.VMEM((2, tq, w), BF16),
                        pltpu.VMEM((2, w, tq), F32),
                        pltpu.VMEM((2, 1, tq), F32),
                        pltpu.VMEM((2, 1, tq), F32),
                        pltpu.VMEM((tk, tq), F32)])
    return pl.pallas_call(
        functools.partial(_diff_kernel, lambda_init=lambda_init),
        grid_spec=grid_spec,
        out_shape=jax.ShapeDtypeStruct((b, s, DIFF_HEADS * w), BF16),
        compiler_params=_params(3),
        name="diff_attn",
    )(slopes, qb, kb, vbt, lq1, lk1, lq2, lk2, g)


def _outproj_kernel(ya_ref, yb_ref, x_ref, w_ref, g_ref, o_ref):
    ka = ya_ref.shape[-1]
    o = jnp.dot(ya_ref[...], w_ref[0:ka, :], preferred_element_type=F32)
    o = o + jnp.dot(yb_ref[...], w_ref[ka:, :], preferred_element_type=F32)
    o_ref[...] = x_ref[...] + _rms(o, g_ref[...])


def _outproj(ya, yb, x2, w_out, g):
    t, d = x2.shape
    tm = PROJ_ROWS
    return pl.pallas_call(
        _outproj_kernel,
        grid=(t // tm,),
        in_specs=[pl.BlockSpec((tm, ya.shape[-1]), lambda i: (i, 0)),
                  pl.BlockSpec((tm, yb.shape[-1]), lambda i: (i, 0)),
                  pl.BlockSpec((tm, d), lambda i: (i, 0)),
                  _const_spec(w_out.shape), _const_spec(g.shape)],
        out_specs=pl.BlockSpec((tm, d), lambda i: (i, 0)),
        out_shape=jax.ShapeDtypeStruct((t, d), F32),
        compiler_params=_params(1),
        name="proj_out",
    )(ya, yb, x2, w_out, g)


def _gelu_tanh(x):
    cdf = 0.5 * (1.0 + jnp.tanh(math.sqrt(2.0 / math.pi) * (x + 0.044715 * (x * x * x))))
    return x * cdf


def _ffn_kernel(x_ref, halo_ref, gpre_ref, wg_ref, wv_ref, cg_ref, cv_ref, wd_ref, gpost_ref,
                o_ref, h_ref, u_ref, acc_ref):
    i = pl.program_id(1)
    ts = x_ref.shape[1]
    nchunk = wg_ref.shape[0]
    x = x_ref[0]
    gpre = gpre_ref[...]
    halo = jnp.where(i == 0, 0.0, _rms(halo_ref[0], gpre))
    h_ref[0:HALO, :] = halo.astype(BF16)
    h_ref[HALO:, :] = _rms(x, gpre).astype(BF16)
    acc_ref[...] = jnp.zeros_like(acc_ref)

    def conv(w_ref, cp_ref, c):
        u_ref[...] = jnp.dot(h_ref[...], w_ref[c], preferred_element_type=F32)
        cp = cp_ref[c]
        out = cp[CONV_WIDTH:CONV_WIDTH + 1]
        for tap in range(CONV_WIDTH):
            start = HALO - (CONV_WIDTH - 1) + tap
            out = out + cp[tap:tap + 1] * u_ref[start:start + ts, :]
        return out

    def chunk(c, carry):
        gate = conv(wg_ref, cg_ref, c)
        val = conv(wv_ref, cv_ref, c)
        act = (_gelu_tanh(gate) * val).astype(BF16)
        acc_ref[...] += jnp.dot(act, wd_ref[c], preferred_element_type=F32)
        return carry

    lax.fori_loop(0, nchunk, chunk, 0)
    o_ref[0] = x + _rms(acc_ref[...], gpost_ref[...])


def _ffn(x3, gpre, wg, wv, cg, cv, wd, gpost):
    b, s, d = x3.shape
    ts = FFN_ROWS
    fc = wg.shape[-1]
    r = ts // HALO
    return pl.pallas_call(
        _ffn_kernel,
        grid=(b, s // ts),
        in_specs=[pl.BlockSpec((1, ts, d), lambda bi, i: (bi, i, 0)),
                  pl.BlockSpec((1, HALO, d), lambda bi, i: (bi, jnp.maximum(i * r - 1, 0), 0)),
                  _const_spec(gpre.shape), _const_spec(wg.shape), _const_spec(wv.shape),
                  _const_spec(cg.shape), _const_spec(cv.shape), _const_spec(wd.shape),
                  _const_spec(gpost.shape)],
        out_specs=pl.BlockSpec((1, ts, d), lambda bi, i: (bi, i, 0)),
        out_shape=jax.ShapeDtypeStruct((b, s, d), F32),
        scratch_shapes=[pltpu.VMEM((ts + HALO, d), BF16),
                        pltpu.VMEM((ts + HALO, fc), F32),
                        pltpu.VMEM((ts, d), F32)],
        compiler_params=_params(2),
        name="conv_ffn",
    )(x3, x3, gpre, wg, wv, cg, cv, wd, gpost)


def _conv_params(conv_w, conv_b, nchunk, fc):
    rows = jnp.concatenate([conv_w, conv_b[None, :]], axis=0)
    rows = jnp.pad(rows, ((0, 8 - rows.shape[0]), (0, 0)))
    return rows.reshape(8, nchunk, fc).transpose(1, 0, 2)


def kernel(x, attn_pre_g, w_in, swa_sinks, swa_out_g, diff_lq1, diff_lk1, diff_lq2, diff_lk2,
           diff_subln_g, w_out, attn_post_g, ffn_pre_g, w_up, conv_w, conv_b, w_down, ffn_post_g):
    batch, seq, d = x.shape
    depth = w_in.shape[0]
    slopes = _alibi_slopes(SWA_Q_HEADS + DIFF_HEADS)
    swa_bias = jnp.asarray(_swa_bias(slopes[:SWA_Q_HEADS]))
    diff_slopes = jnp.asarray(slopes[SWA_Q_HEADS:])
    row = lambda v: v.reshape(1, -1).astype(F32)
    vb_start = w_in.shape[-1] - DIFF_HEADS * DIFF_V_DIM
    d_ff = w_down.shape[1]
    nchunk = d_ff // FFN_CHUNK

    for layer in range(depth):
        lambda_init = 0.8 - 0.6 * math.exp(-0.3 * layer)
        w_main = w_in[layer, :, :vb_start].astype(BF16)
        w_vbt = w_in[layer, :, vb_start:].T.astype(BF16)
        qa, ka, va, qb, kb, vbt = _proj(x.reshape(batch * seq, d), row(attn_pre_g[layer]),
                                        w_main, w_vbt, batch, seq)
        sink_rows = jnp.repeat(swa_sinks[layer].astype(F32), WINDOW).reshape(
            SWA_KV_HEADS, SWA_GROUP * WINDOW, 1)
        ya = _swa(qa.reshape(batch, seq, -1), ka.reshape(batch, seq, -1),
                  va.reshape(batch, seq, -1), swa_bias, sink_rows, row(swa_out_g[layer]))
        yb = _diff(qb.reshape(batch, seq, -1), kb.reshape(batch, seq, -1), vbt, diff_slopes,
                   row(diff_lq1[layer]), row(diff_lk1[layer]), row(diff_lq2[layer]),
                   row(diff_lk2[layer]), row(diff_subln_g[layer]), lambda_init)
        x1 = _outproj(ya.reshape(batch * seq, -1), yb.reshape(batch * seq, -1),
                      x.reshape(batch * seq, d), w_out[layer].astype(BF16),
                      row(attn_post_g[layer]))
        wu = w_up[layer].astype(BF16)
        wg = wu[:, :d_ff].reshape(d, nchunk, FFN_CHUNK).transpose(1, 0, 2)
        wv = wu[:, d_ff:].reshape(d, nchunk, FFN_CHUNK).transpose(1, 0, 2)
        wd = w_down[layer].astype(BF16).reshape(nchunk, FFN_CHUNK, d)
        cg = _conv_params(conv_w[layer][:, :d_ff], conv_b[layer][:d_ff], nchunk, FFN_CHUNK)
        cv = _conv_params(conv_w[layer][:, d_ff:], conv_b[layer][d_ff:], nchunk, FFN_CHUNK)
        x = _ffn(x1.reshape(batch, seq, d), row(ffn_pre_g[layer]), wg, wv, cg, cv, wd,
                 row(ffn_post_g[layer]))
    return x
```

```python
import functools
import math

import numpy as np
import jax
import jax.numpy as jnp
from jax import lax
from jax.experimental import pallas as pl
from jax.experimental.pallas import tpu as pltpu

F32 = jnp.float32
BF16 = jnp.bfloat16

EPS = 1e-6
HEAD_DIM = 64
SWA_Q_HEADS = 8
SWA_KV_HEADS = 2
SWA_GROUP = SWA_Q_HEADS // SWA_KV_HEADS
WINDOW = 128
DIFF_HEADS = 4
DIFF_V_DIM = 2 * HEAD_DIM
CONV_WIDTH = 3
NEG = -1e30

VMEM_LIMIT_BYTES = 56 * 1024 * 1024

PROJ_ROWS = 512
SWA_ROWS = 512
DIFF_TK = 512
DIFF_TQ = 2 * DIFF_TK
LOG2E = math.log2(math.e)
FFN_ROWS = 512
FFN_CHUNK = 512
HALO = 16


def _alibi_slopes(n):
    def pow2(m):
        start = 2.0 ** (-8.0 / m)
        return [start ** (i + 1) for i in range(m)]
    if math.log2(n).is_integer():
        s = pow2(n)
    else:
        c = 2 ** int(math.floor(math.log2(n)))
        s = pow2(c) + pow2(2 * c)[0::2][: n - c]
    return np.array(sorted(s, reverse=True), dtype=np.float32)


def _rms(xf, g):
    return xf * lax.rsqrt(jnp.mean(xf * xf, axis=-1, keepdims=True) + EPS) * g


def _params(n_axes):
    return pltpu.CompilerParams(
        dimension_semantics=("arbitrary",) * n_axes,
        vmem_limit_bytes=VMEM_LIMIT_BYTES)


def _const_spec(shape):
    nd = len(shape)
    return pl.BlockSpec(shape, lambda *_: (0,) * nd, pipeline_mode=pl.Buffered(1))


def _proj_kernel(x_ref, g_ref, w_ref, wvt_ref,
                 qa_ref, ka_ref, va_ref, qb_ref, kb_ref, vbt_ref):
    h = _rms(x_ref[...], g_ref[...]).astype(BF16)
    y = jnp.dot(h, w_ref[...], preferred_element_type=F32)
    scale = HEAD_DIM ** -0.5
    o = 0
    for ref, sc in ((qa_ref, scale), (ka_ref, None), (va_ref, None),
                    (qb_ref, scale * LOG2E), (kb_ref, None)):
        w = ref.shape[-1]
        piece = y[:, o:o + w]
        if sc is not None:
            piece = piece * sc
        ref[...] = piece.astype(BF16)
        o += w
    vt = lax.dot_general(wvt_ref[...], h, (((1,), (1,)), ((), ())),
                         preferred_element_type=F32)
    vbt_ref[0, 0] = vt.astype(BF16)


def _proj(x2, g, w_main, w_vbt, batch, seq):
    t, d = x2.shape
    tm = PROJ_ROWS
    ns = seq // tm
    widths = (SWA_Q_HEADS * HEAD_DIM, SWA_KV_HEADS * HEAD_DIM, SWA_KV_HEADS * HEAD_DIM,
              DIFF_HEADS * 2 * HEAD_DIM, DIFF_HEADS * 2 * HEAD_DIM)
    vw = DIFF_HEADS * DIFF_V_DIM
    out_shape = [jax.ShapeDtypeStruct((t, w), BF16) for w in widths]
    out_shape.append(jax.ShapeDtypeStruct((batch, ns, vw, tm), BF16))
    out_specs = [pl.BlockSpec((tm, w), lambda i: (i, 0)) for w in widths]
    out_specs.append(pl.BlockSpec((1, 1, vw, tm), lambda i: (i // ns, i % ns, 0, 0)))
    return pl.pallas_call(
        _proj_kernel,
        grid=(t // tm,),
        in_specs=[pl.BlockSpec((tm, d), lambda i: (i, 0)),
                  _const_spec(g.shape), _const_spec(w_main.shape), _const_spec(w_vbt.shape)],
        out_specs=out_specs,
        out_shape=out_shape,
        compiler_params=_params(1),
        name="proj_in",
    )(x2, g, w_main, w_vbt)


def _swa_bias(slopes):
    qpos = np.arange(WINDOW)[:, None] + WINDOW
    kpos = np.arange(2 * WINDOW)[None, :]
    dist = (qpos - kpos).astype(np.float32)
    valid = (dist >= 0) & (dist < WINDOW)
    first_valid = valid & (kpos >= WINDOW)
    out = np.empty((2, SWA_KV_HEADS, SWA_GROUP * WINDOW, 2 * WINDOW), np.float32)
    for g in range(SWA_KV_HEADS):
        for j in range(SWA_GROUP):
            sl = slopes[g * SWA_GROUP + j]
            rows = slice(j * WINDOW, (j + 1) * WINDOW)
            out[0, g, rows] = np.where(valid, -sl * dist, NEG)
            out[1, g, rows] = np.where(first_valid, -sl * dist, NEG)
    return out


def _swa_kernel(q_ref, k_ref, kp_ref, v_ref, vp_ref, bias_ref, sink_ref, g_ref, o_ref):
    i = pl.program_id(1)
    nblk = q_ref.shape[1] // WINDOW
    for n in range(nblk):
        rows = slice(n * WINDOW, (n + 1) * WINDOW)
        if n == 0:
            kk_all = jnp.concatenate([kp_ref[0], k_ref[0, rows, :]], axis=0)
            vv_all = jnp.concatenate([vp_ref[0], v_ref[0, rows, :]], axis=0)
        else:
            kk_all = k_ref[0, (n - 1) * WINDOW:(n + 1) * WINDOW, :]
            vv_all = v_ref[0, (n - 1) * WINDOW:(n + 1) * WINDOW, :]
        q_all = q_ref[0, rows, :]
        outs = []
        for g in range(SWA_KV_HEADS):
            kk = kk_all[:, g * HEAD_DIM:(g + 1) * HEAD_DIM]
            vv = vv_all[:, g * HEAD_DIM:(g + 1) * HEAD_DIM]
            qs = jnp.concatenate(
                [q_all[:, (g * SWA_GROUP + j) * HEAD_DIM:(g * SWA_GROUP + j + 1) * HEAD_DIM]
                 for j in range(SWA_GROUP)], axis=0)
            s = lax.dot_general(qs, kk, (((1,), (1,)), ((), ())),
                                preferred_element_type=F32)
            if n == 0:
                bias = jnp.where(i == 0, bias_ref[1, g], bias_ref[0, g])
            else:
                bias = bias_ref[0, g]
            s = s + bias
            sink = sink_ref[g]
            m = jnp.maximum(jnp.max(s, axis=-1, keepdims=True), sink)
            p = jnp.exp(s - m)
            l = jnp.sum(p, axis=-1, keepdims=True) + jnp.exp(sink - m)
            o = jnp.dot(p.astype(BF16), vv, preferred_element_type=F32) / l
            outs.extend(o[j * WINDOW:(j + 1) * WINDOW] for j in range(SWA_GROUP))
        y = jnp.concatenate(outs, axis=-1)
        o_ref[0, rows, :] = _rms(y, g_ref[...]).astype(BF16)


def _swa(qa, ka, va, bias, sink_rows, g):
    b, s, qw = qa.shape
    kw = ka.shape[-1]
    tq = SWA_ROWS
    r = tq // WINDOW
    cur = lambda bi, i: (bi, i, 0)
    prev = lambda bi, i: (bi, jnp.maximum(i * r - 1, 0), 0)
    return pl.pallas_call(
        _swa_kernel,
        grid=(b, s // tq),
        in_specs=[pl.BlockSpec((1, tq, qw), cur),
                  pl.BlockSpec((1, tq, kw), cur), pl.BlockSpec((1, WINDOW, kw), prev),
                  pl.BlockSpec((1, tq, kw), cur), pl.BlockSpec((1, WINDOW, kw), prev),
                  _const_spec(bias.shape), _const_spec(sink_rows.shape), _const_spec(g.shape)],
        out_specs=pl.BlockSpec((1, tq, qw), cur),
        out_shape=jax.ShapeDtypeStruct((b, s, qw), BF16),
        compiler_params=_params(2),
        name="swa_attn",
    )(qa, ka, ka, va, va, bias, sink_rows, g)


N_BIAS_COLS = 3


def _diff_kernel(slopes_ref, q_ref, k_ref, vt_ref, lq1_ref, lk1_ref, lq2_ref, lk2_ref, g_ref,
                 o_ref, qx_ref, kf_ref, sa_ref, sb_ref, acc_ref, m_ref, l_ref, *, lambda_init):
    h = pl.program_id(1)
    i = pl.program_id(2)
    tq = q_ref.shape[1]
    tk = kf_ref.shape[0]
    w = q_ref.shape[2]
    slope2 = slopes_ref[h] * LOG2E
    assert tq == 2 * tk

    @pl.when(i == 0)
    def _():
        kloc = lax.broadcasted_iota(jnp.int32, (tk, w), 0).astype(F32)
        col = lax.broadcasted_iota(jnp.int32, (tk, w), 1)
        rest = slope2 * kloc
        feat = jnp.zeros((tk, w), F32)
        for n in range(N_BIAS_COLS):
            part = rest.astype(BF16).astype(F32)
            feat = jnp.where(col == n, part, feat)
            rest = rest - part
        kf_ref[...] = feat.astype(BF16)

    q = q_ref[0]
    lane = lax.broadcasted_iota(jnp.int32, q.shape, 1)
    zero = jnp.zeros_like(q)
    ones_cols = jnp.where(lane < N_BIAS_COLS, 1.0, 0.0).astype(BF16)
    qx_ref[0] = jnp.concatenate([jnp.where(lane < HEAD_DIM, q, zero), ones_cols], axis=1)
    qx_ref[1] = jnp.concatenate([jnp.where(lane >= HEAD_DIM, q, zero), ones_cols], axis=1)
    acc_ref[...] = jnp.zeros_like(acc_ref)
    l_ref[...] = jnp.zeros_like(l_ref)
    m_ref[...] = jnp.full_like(m_ref, NEG)

    def scores(j, s_ref):
        kt = k_ref[0, pl.ds(pl.multiple_of(j * tk, tk), tk), :]
        kx = jnp.concatenate([kt, kf_ref[...]], axis=1)
        for mp in range(2):
            s_ref[mp] = lax.dot_general(kx, qx_ref[mp], (((1,), (1,)), ((), ())),
                                        preferred_element_type=F32)

    def softmax_pv(j, s_ref, diag_offset):
        vt = vt_ref[0, j]
        c = slope2 * (j * tk).astype(F32)
        for mp in range(2):
            s = s_ref[mp]
            if diag_offset is not None:
                krow = lax.broadcasted_iota(jnp.int32, (tk, tq), 0) + diag_offset
                qcol = lax.broadcasted_iota(jnp.int32, (tk, tq), 1)
                s = jnp.where(krow <= qcol, s, NEG)
            m_old = m_ref[mp]
            m_new = jnp.maximum(m_old, jnp.max(s, axis=0, keepdims=True) + c)
            alpha = jnp.exp2(m_old - m_new)
            p = jnp.exp2(s - (m_new - c))
            l_ref[mp] = alpha * l_ref[mp] + jnp.sum(p, axis=0, keepdims=True)
            acc_ref[mp] = alpha * acc_ref[mp] + jnp.dot(
                vt, p.astype(BF16), preferred_element_type=F32)
            m_ref[mp] = m_new

    scores(0, sa_ref)

    def pair(t, carry):
        j = 2 * t
        scores(j + 1, sb_ref)
        softmax_pv(j, sa_ref, None)
        scores(j + 2, sa_ref)
        softmax_pv(j + 1, sb_ref, None)
        return carry

    lax.fori_loop(0, i, pair, 0)
    scores(2 * i + 1, sb_ref)
    softmax_pv(2 * i, sa_ref, 0)
    softmax_pv(2 * i + 1, sb_ref, tk)

    lam =(jnp.exp(jnp.sum(lq1_ref[...] * lk1_ref[...], axis=-1, keepdims=True))
           - jnp.exp(jnp.sum(lq2_ref[...] * lk2_ref[...], axis=-1, keepdims=True))
           + lambda_init)
    y = acc_ref[0] * (1.0 / l_ref[0]) - lam * (acc_ref[1] * (1.0 / l_ref[1]))
    yt = y.T
    o_ref[0] = (_rms(yt, g_ref[...]) * (1.0 - lambda_init)).astype(BF16)


def _diff(qb, kb, vbt, slopes, lq1, lk1, lq2, lk2, g, lambda_init):
    b, s, _ = qb.shape
    tq, tk = DIFF_TQ, DIFF_TK
    w = DIFF_V_DIM
    ntk = vbt.shape[1]
    grid_spec = pltpu.PrefetchScalarGridSpec(
        num_scalar_prefetch=1,
        grid=(b, DIFF_HEADS, s // tq),
        in_specs=[pl.BlockSpec((1, tq, w), lambda bi, h, i, sl: (bi, i, h)),
                  pl.BlockSpec((1, s, w), lambda bi, h, i, sl: (bi, 0, h)),
                  pl.BlockSpec((1, ntk, w, tk), lambda bi, h, i, sl: (bi, 0, h, 0)),
                  _const_spec(lq1.shape), _const_spec(lk1.shape),
                  _const_spec(lq2.shape), _const_spec(lk2.shape), _const_spec(g.shape)],
        out_specs=pl.BlockSpec((1, tq, w), lambda bi, h, i, sl: (bi, i, h)),
        scratch_shapes=[pltpu.VMEM((2, tq, 2 * w), BF16),
                        pltpu.VMEM((tk, w), BF16),
                        pltpu.VMEM((2, tk, tq), F32),
                        pltpu.VMEM((2, tk, tq), F32),
                        pltpu.VMEM((2, w, tq), F32),
                        pltpu.VMEM((2, 1, tq), F32),
                        pltpu.VMEM((2, 1, tq), F32)])
    return pl.pallas_call(
        functools.partial(_diff_kernel, lambda_init=lambda_init),
        grid_spec=grid_spec,
        out_shape=jax.ShapeDtypeStruct((b, s, DIFF_HEADS * w), BF16),
        compiler_params=_params(3),
        name="diff_attn",
    )(slopes, qb, kb, vbt, lq1, lk1, lq2, lk2, g)


def _outproj_kernel(ya_ref, yb_ref, x_ref, w_ref, g_ref, o_ref):
    ka = ya_ref.shape[-1]
    o = jnp.dot(ya_ref[...], w_ref[0:ka, :], preferred_element_type=F32)
    o = o + jnp.dot(yb_ref[...], w_ref[ka:, :], preferred_element_type=F32)
    o_ref[...] = x_ref[...] + _rms(o, g_ref[...])


def _outproj(ya, yb, x2, w_out, g):
    t, d = x2.shape
    tm = PROJ_ROWS
    return pl.pallas_call(
        _outproj_kernel,
        grid=(t // tm,),
        in_specs=[pl.BlockSpec((tm, ya.shape[-1]), lambda i: (i, 0)),
                  pl.BlockSpec((tm, yb.shape[-1]), lambda i: (i, 0)),
                  pl.BlockSpec((tm, d), lambda i: (i, 0)),
                  _const_spec(w_out.shape), _const_spec(g.shape)],
        out_specs=pl.BlockSpec((tm, d), lambda i: (i, 0)),
        out_shape=jax.ShapeDtypeStruct((t, d), F32),
        compiler_params=_params(1),
        name="proj_out",
    )(ya, yb, x2, w_out, g)


def _gelu_tanh(x):
    cdf = 0.5 * (1.0 + jnp.tanh(math.sqrt(2.0 / math.pi) * (x + 0.044715 * (x * x * x))))
    return x * cdf


def _ffn_kernel(x_ref, halo_ref, gpre_ref, wg_ref, wv_ref, cg_ref, cv_ref, wd_ref, gpost_ref,
                o_ref, h_ref, u_ref, acc_ref):
    i = pl.program_id(1)
    ts = x_ref.shape[1]
    nchunk = wg_ref.shape[0]
    x = x_ref[0]
    gpre = gpre_ref[...]
    halo = jnp.where(i == 0, 0.0, _rms(halo_ref[0], gpre))
    h_ref[0:HALO, :] = halo.astype(BF16)
    h_ref[HALO:, :] = _rms(x, gpre).astype(BF16)
    acc_ref[...] = jnp.zeros_like(acc_ref)

    def conv(w_ref, cp_ref, c):
        u_ref[...] = jnp.dot(h_ref[...], w_ref[c], preferred_element_type=F32)
        cp = cp_ref[c]
        out = cp[CONV_WIDTH:CONV_WIDTH + 1]
        for tap in range(CONV_WIDTH):
            start = HALO - (CONV_WIDTH - 1) + tap
            out = out + cp[tap:tap + 1] * u_ref[start:start + ts, :]
        return out

    def chunk(c, carry):
        gate = conv(wg_ref, cg_ref, c)
        val = conv(wv_ref, cv_ref, c)
        act = (_gelu_tanh(gate) * val).astype(BF16)
        acc_ref[...] += jnp.dot(act, wd_ref[c], preferred_element_type=F32)
        return carry

    lax.fori_loop(0, nchunk, chunk, 0)
    o_ref[0] = x + _rms(acc_ref[...], gpost_ref[...])


def _ffn(x3, gpre, wg, wv, cg, cv, wd, gpost):
    b, s, d = x3.shape
    ts = FFN_ROWS
    fc = wg.shape[-1]
    r = ts // HALO
    return pl.pallas_call(
        _ffn_kernel,
        grid=(b, s // ts),
        in_specs=[pl.BlockSpec((1, ts, d), lambda bi, i: (bi, i, 0)),
                  pl.BlockSpec((1, HALO, d), lambda bi, i: (bi, jnp.maximum(i * r - 1, 0), 0)),
                  _const_spec(gpre.shape), _const_spec(wg.shape), _const_spec(wv.shape),
                  _const_spec(cg.shape), _const_spec(cv.shape), _const_spec(wd.shape),
                  _const_spec(gpost.shape)],
        out_specs=pl.BlockSpec((1, ts, d), lambda bi, i: (bi, i, 0)),
        out_shape=jax.ShapeDtypeStruct((b, s, d), F32),
        scratch_shapes=[pltpu.VMEM((ts + HALO, d), BF16),
                        pltpu.VMEM((ts + HALO, fc), F32),
                        pltpu.VMEM((ts, d), F32)],
        compiler_params=_params(2),
        name="conv_ffn",
    )(x3, x3, gpre, wg, wv, cg, cv, wd, gpost)


def _conv_params(conv_w, conv_b, nchunk, fc):
    rows = jnp.concatenate([conv_w, conv_b[None, :]], axis=0)
    rows = jnp.pad(rows, ((0, 8 - rows.shape[0]), (0, 0)))
    return rows.reshape(8, nchunk, fc).transpose(1, 0, 2)


def kernel(x, attn_pre_g, w_in, swa_sinks, swa_out_g, diff_lq1, diff_lk1, diff_lq2, diff_lk2,
           diff_subln_g, w_out, attn_post_g, ffn_pre_g, w_up, conv_w, conv_b, w_down, ffn_post_g):
    batch, seq, d = x.shape
    depth = w_in.shape[0]
    slopes = _alibi_slopes(SWA_Q_HEADS + DIFF_HEADS)
    swa_bias = jnp.asarray(_swa_bias(slopes[:SWA_Q_HEADS]))
    diff_slopes = jnp.asarray(slopes[SWA_Q_HEADS:])
    row = lambda v: v.reshape(1, -1).astype(F32)
    vb_start = w_in.shape[-1] - DIFF_HEADS * DIFF_V_DIM
    d_ff = w_down.shape[1]
    nchunk = d_ff // FFN_CHUNK

    for layer in range(depth):
        lambda_init = 0.8 - 0.6 * math.exp(-0.3 * layer)
        w_main = w_in[layer, :, :vb_start].astype(BF16)
        w_vbt = w_in[layer, :, vb_start:].T.astype(BF16)
        qa, ka, va, qb, kb, vbt = _proj(x.reshape(batch * seq, d), row(attn_pre_g[layer]),
                                        w_main, w_vbt, batch, seq)
        sink_rows = jnp.repeat(swa_sinks[layer].astype(F32), WINDOW).reshape(
            SWA_KV_HEADS, SWA_GROUP * WINDOW, 1)
        ya = _swa(qa.reshape(batch, seq, -1), ka.reshape(batch, seq, -1),
                  va.reshape(batch, seq, -1), swa_bias, sink_rows, row(swa_out_g[layer]))
        yb = _diff(qb.reshape(batch, seq, -1), kb.reshape(batch, seq, -1), vbt, diff_slopes,
                   row(diff_lq1[layer]), row(diff_lk1[layer]), row(diff_lq2[layer]),
                   row(diff_lk2[layer]), row(diff_subln_g[layer]), lambda_init)
        x1 = _outproj(ya.reshape(batch * seq, -1), yb.reshape(batch * seq, -1),
                      x.reshape(batch * seq, d), w_out[layer].astype(BF16),
                      row(attn_post_g[layer]))
        wu = w_up[layer].astype(BF16)
        wg = wu[:, :d_ff].reshape(d, nchunk, FFN_CHUNK).transpose(1, 0, 2)
        wv = wu[:, d_ff:].reshape(d, nchunk, FFN_CHUNK).transpose(1, 0, 2)
        wd = w_down[layer].astype(BF16).reshape(nchunk, FFN_CHUNK, d)
        cg = _conv_params(conv_w[layer][:, :d_ff], conv_b[layer][:d_ff], nchunk, FFN_CHUNK)
        cv = _conv_params(conv_w[layer][:, d_ff:], conv_b[layer][d_ff:], nchunk, FFN_CHUNK)
        x = _ffn(x1.reshape(batch, seq, d), row(ffn_pre_g[layer]), wg, wv, cg, cv, wd,
                 row(ffn_post_g[layer]))
    return x
```

```python
import functools
import math

import numpy as np
import jax
import jax.numpy as jnp
from jax import lax
from jax.experimental import pallas as pl
from jax.experimental.pallas import tpu as pltpu

F32 = jnp.float32
BF16 = jnp.bfloat16

EPS = 1e-6
HEAD_DIM = 64
SWA_Q_HEADS = 8
SWA_KV_HEADS = 2
SWA_GROUP = SWA_Q_HEADS // SWA_KV_HEADS
WINDOW = 128
DIFF_HEADS = 4
DIFF_V_DIM = 2 * HEAD_DIM
CONV_WIDTH = 3
NEG = -1e30
LOG2E = math.log2(math.e)

LANES = 128
SUBLANES = 8
VMEM_LIMIT_BYTES = 56 * 1024 * 1024

PROJ_ROWS = 512
SWA_ROWS = 512
DIFF_TK = 512
DIFF_TQ = 2 * DIFF_TK
FFN_ROWS = 512
FFN_CHUNK = 512
HALO = 16


def _alibi_slopes(n):
    def pow2(m):
        start = 2.0 ** (-8.0 / m)
        return [start ** (i + 1) for i in range(m)]
    if math.log2(n).is_integer():
        s = pow2(n)
    else:
        c = 2 ** int(math.floor(math.log2(n)))
        s = pow2(c) + pow2(2 * c)[0::2][: n - c]
    return np.array(sorted(s, reverse=True), dtype=np.float32)


def _rms(xf, g):
    return xf * lax.rsqrt(jnp.mean(xf * xf, axis=-1, keepdims=True) + EPS) * g


def _params(n_axes, flags=None):
    return pltpu.CompilerParams(
        dimension_semantics=("arbitrary",) * n_axes,
        vmem_limit_bytes=VMEM_LIMIT_BYTES,
        flags=flags)


def _const_spec(shape):
    nd = len(shape)
    return pl.BlockSpec(shape, lambda *_: (0,) * nd, pipeline_mode=pl.Buffered(1))


def _proj_kernel(x_ref, g_ref, w_ref, wvt_ref, qa_ref, ka_ref, qb_ref, kb_ref, vat_ref, vbt_ref):
    h = _rms(x_ref[...], g_ref[...]).astype(BF16)
    y = jnp.dot(h, w_ref[...], preferred_element_type=F32)
    scale = HEAD_DIM ** -0.5 * LOG2E
    o = 0
    for ref, sc in ((qa_ref, scale), (ka_ref, None), (qb_ref, scale), (kb_ref, None)):
        w = ref.shape[-1]
        piece = y[:, o:o + w]
        if sc is not None:
            piece = piece * sc
        ref[...] = piece.astype(BF16)
        o += w
    vt = lax.dot_general(wvt_ref[...], h, (((1,), (1,)), ((), ())),
                         preferred_element_type=F32).astype(BF16)
    va_rows = vat_ref.shape[1]
    vat_ref[0] = vt[:va_rows]
    vbt_ref[0, 0] = vt[va_rows:]


def _proj(x2, g, w_main, w_vt, batch, seq):
    t, d = x2.shape
    tm = PROJ_ROWS
    ns = seq // tm
    qa_w = SWA_Q_HEADS * HEAD_DIM
    ka_w = SWA_KV_HEADS * HEAD_DIM
    qb_w = DIFF_HEADS * 2 * HEAD_DIM
    vb_w = DIFF_HEADS * DIFF_V_DIM
    widths = (qa_w, ka_w, qb_w, qb_w)
    out_shape = [jax.ShapeDtypeStruct((t, w), BF16) for w in widths]
    out_shape.append(jax.ShapeDtypeStruct((batch, ka_w, seq), BF16))
    out_shape.append(jax.ShapeDtypeStruct((batch, ns, vb_w, tm), BF16))
    out_specs = [pl.BlockSpec((tm, w), lambda i: (i, 0)) for w in widths]
    out_specs.append(pl.BlockSpec((1, ka_w, tm), lambda i: (i // ns, 0, i % ns)))
    out_specs.append(pl.BlockSpec((1, 1, vb_w, tm), lambda i: (i // ns, i % ns, 0, 0)))
    return pl.pallas_call(
        _proj_kernel,
        grid=(t // tm,),
        in_specs=[pl.BlockSpec((tm, d), lambda i: (i, 0)),
                  _const_spec(g.shape), _const_spec(w_main.shape), _const_spec(w_vt.shape)],
        out_specs=out_specs,
        out_shape=out_shape,
        compiler_params=_params(1),
        name="proj_in",
    )(x2, g, w_main, w_vt)


def _swa_q_columns():
    cols = []
    for t in range(SWA_GROUP):
        for g in range(SWA_KV_HEADS):
            head = g * SWA_GROUP + t
            cols.extend(range(head * HEAD_DIM, (head + 1) * HEAD_DIM))
    return np.array(cols, dtype=np.int32)


def _swa_bias(slopes):
    qpos = np.arange(WINDOW)[None, :] + WINDOW
    kpos = np.arange(2 * WINDOW)[:, None]
    dist = (qpos - kpos).astype(np.float32)
    valid = (dist >= 0) & (dist < WINDOW)
    first_valid = valid & (kpos >= WINDOW)
    out = np.empty((2, SWA_KV_HEADS, 2 * WINDOW, SWA_GROUP * WINDOW), np.float32)
    for g in range(SWA_KV_HEADS):
        for t in range(SWA_GROUP):
            sl = np.float32(slopes[g * SWA_GROUP + t] * LOG2E)
            cols = slice(t * WINDOW, (t + 1) * WINDOW)
            out[0, g, :, cols] = np.where(valid, -sl * dist, NEG)
            out[1, g, :, cols] = np.where(first_valid, -sl * dist, NEG)
    return out


def _swa_kernel(q_ref, k_ref, kp_ref, vt_ref, vtp_ref, bias_ref, sink_ref, g_ref, o_ref):
    i = pl.program_id(1)
    nblk = q_ref.shape[1] // WINDOW
    lane = lax.broadcasted_iota(jnp.int32, (WINDOW, LANES), 1)
    for n in range(nblk):
        rows = slice(n * WINDOW, (n + 1) * WINDOW)
        if n == 0:
            kk = jnp.concatenate([kp_ref[0], k_ref[0, rows, :]], axis=0)
            vvt = jnp.concatenate([vtp_ref[0], vt_ref[0, :, rows]], axis=1)
        else:
            kk = k_ref[0, (n - 1) * WINDOW:(n + 1) * WINDOW, :]
            vvt = vt_ref[0, :, (n - 1) * WINDOW:(n + 1) * WINDOW]
        q_tiles = [q_ref[0, rows, t * LANES:(t + 1) * LANES] for t in range(SWA_GROUP)]
        pieces = []
        for g in range(SWA_KV_HEADS):
            keep = (lane < HEAD_DIM) if g == 0 else (lane >= HEAD_DIM)
            qm = jnp.concatenate([jnp.where(keep, qt, jnp.zeros_like(qt)) for qt in q_tiles],
                                 axis=0)
            s = lax.dot_general(kk, qm, (((1,), (1,)), ((), ())),
                                preferred_element_type=F32)
            if n == 0:
                bias = jnp.where(i == 0, bias_ref[1, g], bias_ref[0, g])
            else:
                bias = bias_ref[0, g]
            s = s + bias
            sink = sink_ref[g]
            m = jnp.maximum(jnp.max(s, axis=0, keepdims=True), sink)
            p = jnp.exp2(s - m)
            l = jnp.sum(p, axis=0, keepdims=True) + jnp.exp2(sink - m)
            ot = jnp.dot(vvt[g * HEAD_DIM:(g + 1) * HEAD_DIM], p.astype(BF16),
                         preferred_element_type=F32) * (1.0 / l)
            pieces.extend(ot[:, t * WINDOW:(t + 1) * WINDOW] for t in range(SWA_GROUP))
        y = jnp.concatenate(pieces, axis=0).T
        o_ref[0, rows, :] = _rms(y, g_ref[...]).astype(BF16)


def _swa(qa, ka, vat, bias, sink_rows, g):
    b, s, qw = qa.shape
    kw = ka.shape[-1]
    tq = SWA_ROWS
    r = tq // WINDOW
    cur = lambda bi, i: (bi, i, 0)
    prev = lambda bi, i: (bi, jnp.maximum(i * r - 1, 0), 0)
    cur_t = lambda bi, i: (bi, 0, i)
    prev_t = lambda bi, i: (bi, 0, jnp.maximum(i * r - 1, 0))
    return pl.pallas_call(
        _swa_kernel,
        grid=(b, s // tq),
        in_specs=[pl.BlockSpec((1, tq, qw), cur),
                  pl.BlockSpec((1, tq, kw), cur), pl.BlockSpec((1, WINDOW, kw), prev),
                  pl.BlockSpec((1, kw, tq), cur_t), pl.BlockSpec((1, kw, WINDOW), prev_t),
                  _const_spec(bias.shape), _const_spec(sink_rows.shape), _const_spec(g.shape)],
        out_specs=pl.BlockSpec((1, tq, qw), cur),
        out_shape=jax.ShapeDtypeStruct((b, s, qw), BF16),
        compiler_params=_params(2),
        name="swa_attn",
    )(qa, ka, ka, vat, vat, bias, sink_rows, g)


N_BIAS_COLS = 3
DEN_ROWS = 16
M_INIT = -1e28
DEN_LIMIT = 2.0 ** 24


def _diff_kernel(slopes_ref, q_ref, k_ref, vt_ref, lq1_ref, lk1_ref, lq2_ref, lk2_ref, g_ref,
                 o_ref, qx_ref, kf_ref, pa_ref, pb_ref, aa_ref, ab_ref, acc_ref, m_ref,
                 *, lambda_init):
    h = pl.program_id(1)
    i = pl.program_id(2)
    tq = q_ref.shape[1]
    tk = kf_ref.shape[0]
    w = q_ref.shape[2]
    slope2 = slopes_ref[h] * LOG2E
    assert tq == 2 * tk

    @pl.when(i == 0)
    def _():
        kloc = lax.broadcasted_iota(jnp.int32, (tk, w), 0).astype(F32)
        col = lax.broadcasted_iota(jnp.int32, (tk, w), 1)
        rest = slope2 * kloc
        feat = jnp.zeros((tk, w), F32)
        for n in range(N_BIAS_COLS):
            part = rest.astype(BF16).astype(F32)
            feat = jnp.where(col == n, part, feat)
            rest = rest - part
        kf_ref[...] = feat.astype(BF16)

    q = q_ref[0]
    lane = lax.broadcasted_iota(jnp.int32, q.shape, 1)
    zero = jnp.zeros_like(q)
    ones_cols = jnp.where(lane < N_BIAS_COLS, 1.0, 0.0).astype(BF16)
    qx_ref[0] = jnp.concatenate([jnp.where(lane < HEAD_DIM, q, zero), ones_cols], axis=1)
    qx_ref[1] = jnp.concatenate([jnp.where(lane >= HEAD_DIM, q, zero), ones_cols], axis=1)
    ones_rows = jnp.ones((DEN_ROWS, tk), BF16)

    def softmax(j, p_ref, alpha_ref, diag_offset, track_max):
        kt = k_ref[0, pl.ds(pl.multiple_of(j * tk, tk), tk), :]
        kx = jnp.concatenate([kt, kf_ref[...]], axis=1)
        c = slope2 * (j * tk).astype(F32)
        for mp in range(2):
            s = lax.dot_general(kx, qx_ref[mp], (((1,), (1,)), ((), ())),
                                preferred_element_type=F32)
            if diag_offset is not None:
                krow = lax.broadcasted_iota(jnp.int32, (tk, tq), 0) + diag_offset
                qcol = lax.broadcasted_iota(jnp.int32, (tk, tq), 1)
                s = jnp.where(krow <= qcol, s, NEG)
            m_old = m_ref[mp]
            if track_max:
                m_new = jnp.maximum(m_old, jnp.max(s, axis=0, keepdims=True) + c)
                alpha_ref[mp] = jnp.exp2(m_old - m_new)
                m_ref[mp] = m_new
            else:
                m_new = m_old
                alpha_ref[mp] = jnp.ones_like(m_old)
            p_ref[mp] = jnp.exp2(s - (m_new - c)).astype(BF16)

    def weighted_values(j, p_ref, alpha_ref):
        vx = jnp.concatenate([vt_ref[0, j], ones_rows], axis=0)
        for mp in range(2):
            acc_ref[mp] = alpha_ref[mp] * acc_ref[mp] + jnp.dot(
                vx, p_ref[mp], preferred_element_type=F32)

    def attend(track_all):
        acc_ref[...] = jnp.zeros_like(acc_ref)
        m_ref[...] = jnp.full_like(m_ref, M_INIT)
        last = 2 * i + 1
        softmax(last, pa_ref, aa_ref, tk, True)
        softmax(last - 1, pb_ref, ab_ref, 0, True)
        weighted_values(last, pa_ref, aa_ref)

        def pair(t, carry):
            j = last - 2 - 2 * t
            softmax(j, pa_ref, aa_ref, None, track_all)
            weighted_values(j + 1, pb_ref, ab_ref)
            softmax(j - 1, pb_ref, ab_ref, None, track_all)
            weighted_values(j, pa_ref, aa_ref)
            return carry

        lax.fori_loop(0, i, pair, 0)
        weighted_values(0, pb_ref, ab_ref)

    attend(False)
    den_max = jnp.max(jnp.maximum(acc_ref[0, w:w + 1, :], acc_ref[1, w:w + 1, :]))

    @pl.when(jnp.logical_not(den_max <= DEN_LIMIT))
    def _():
        attend(True)

    lam =(jnp.exp(jnp.sum(lq1_ref[...] * lk1_ref[...], axis=-1, keepdims=True))
           - jnp.exp(jnp.sum(lq2_ref[...] * lk2_ref[...], axis=-1, keepdims=True))
           + lambda_init)
    y = (acc_ref[0, 0:w, :] * (1.0 / acc_ref[0, w:w + 1, :])
         - lam * (acc_ref[1, 0:w, :] * (1.0 / acc_ref[1, w:w + 1, :])))
    yt = y.T
    o_ref[0] = (_rms(yt, g_ref[...]) * (1.0 - lambda_init)).astype(BF16)


def _diff(qb, kb, vbt, slopes, lq1, lk1, lq2, lk2, g, lambda_init):
    b, s, _ = qb.shape
    tq, tk = DIFF_TQ, DIFF_TK
    w = DIFF_V_DIM
    ntk = vbt.shape[1]
    grid_spec = pltpu.PrefetchScalarGridSpec(
        num_scalar_prefetch=1,
        grid=(b, DIFF_HEADS, s // tq),
        in_specs=[pl.BlockSpec((1, tq, w), lambda bi, h, i, sl: (bi, i, h)),
                  pl.BlockSpec((1, s, w), lambda bi, h, i, sl: (bi, 0, h)),
                  pl.BlockSpec((1, ntk, w, tk), lambda bi, h, i, sl: (bi, 0, h, 0)),
                  _const_spec(lq1.shape), _const_spec(lk1.shape),
                  _const_spec(lq2.shape), _const_spec(lk2.shape), _const_spec(g.shape)],
        out_specs=pl.BlockSpec((1, tq, w), lambda bi, h, i, sl: (bi, i, h)),
        scratch_shapes=[pltpu.VMEM((2, tq, 2 * w), BF16),
                        pltpu.VMEM((tk, w), BF16),
                        pltpu.VMEM((2, tk, tq), BF16),
                        pltpu.VMEM((2, tk, tq), BF16),
                        pltpu.VMEM((2, 1, tq), F32),
                        pltpu.VMEM((2, 1, tq), F32),
                        pltpu.VMEM((2, w + DEN_ROWS, tq), F32),
                        pltpu.VMEM((2, 1, tq), F32)])
    return pl.pallas_call(
        functools.partial(_diff_kernel, lambda_init=lambda_init),
        grid_spec=grid_spec,
        out_shape=jax.ShapeDtypeStruct((b, s, DIFF_HEADS * w), BF16),
        compiler_params=_params(3),
        name="diff_attn",
    )(slopes, qb, kb, vbt, lq1, lk1, lq2, lk2, g)


def _outproj_kernel(ya_ref, yb_ref, x_ref, w_ref, g_ref, o_ref):
    ka = ya_ref.shape[-1]
    o = jnp.dot(ya_ref[...], w_ref[0:ka, :], preferred_element_type=F32)
    o = o + jnp.dot(yb_ref[...], w_ref[ka:, :], preferred_element_type=F32)
    o_ref[...] = x_ref[...] + _rms(o, g_ref[...])


def _outproj(ya, yb, x2, w_out, g):
    t, d = x2.shape
    tm = PROJ_ROWS
    return pl.pallas_call(
        _outproj_kernel,
        grid=(t // tm,),
        in_specs=[pl.BlockSpec((tm, ya.shape[-1]), lambda i: (i, 0)),
                  pl.BlockSpec((tm, yb.shape[-1]), lambda i: (i, 0)),
                  pl.BlockSpec((tm, d), lambda i: (i, 0)),
                  _const_spec(w_out.shape), _const_spec(g.shape)],
        out_specs=pl.BlockSpec((tm, d), lambda i: (i, 0)),
        out_shape=jax.ShapeDtypeStruct((t, d), F32),
        compiler_params=_params(1),
        name="proj_out",
    )(ya, yb, x2, w_out, g)


def _gelu_tanh(x):
    cdf = 0.5 * (1.0 + jnp.tanh(math.sqrt(2.0 / math.pi) * (x + 0.044715 * (x * x * x))))
    return x * cdf


def _ffn_kernel(x_ref, halo_ref, gpre_ref, wup_ref, cw_ref, wd_ref, gpost_ref,
                o_ref, h_ref, u_ref, acta_ref, actb_ref, acc_ref):
    i = pl.program_id(1)
    ts = x_ref.shape[1]
    d_ff = wd_ref.shape[0]
    fc = u_ref.shape[-1]
    nchunk = d_ff // fc
    assert nchunk % 2 == 0

    gpre = gpre_ref[...]
    h_ref[0:HALO, :] = jnp.where(i == 0, 0.0, _rms(halo_ref[0], gpre)).astype(BF16)
    h_ref[HALO:, :] = _rms(x_ref[0], gpre).astype(BF16)

    def cols(c, half):
        return pl.ds(pl.multiple_of(half * d_ff + c * fc, fc), fc)

    def up_act(c, act_ref):
        hh = h_ref[...]
        outs = []
        for half in range(2):
            u_ref[half] = jnp.dot(hh, wup_ref[:, cols(c, half)],
                                  preferred_element_type=F32)
            cp = cw_ref[:, cols(c, half)]
            out = cp[CONV_WIDTH:CONV_WIDTH + 1]
            for tap in range(CONV_WIDTH):
                start = HALO - (CONV_WIDTH - 1) + tap
                out = out + cp[tap:tap + 1] * u_ref[half, start:start + ts, :]
            outs.append(out)
        act_ref[...] = (_gelu_tanh(outs[0]) * outs[1]).astype(BF16)

    def down(c, act_ref):
        rows = pl.ds(pl.multiple_of(c * fc, fc), fc)
        acc_ref[...] += jnp.dot(act_ref[...], wd_ref[rows, :], preferred_element_type=F32)

    acc_ref[...] = jnp.zeros_like(acc_ref)
    up_act(0, acta_ref)

    def body(t, carry):
        c = 2 * t + 1
        up_act(c, actb_ref)
        down(c - 1, acta_ref)
        up_act(c + 1, acta_ref)
        down(c, actb_ref)
        return carry

    lax.fori_loop(0, (nchunk - 2) // 2, body, 0)
    up_act(nchunk - 1, actb_ref)
    down(nchunk - 2, acta_ref)
    down(nchunk - 1, actb_ref)

    o_ref[0] = x_ref[0] + _rms(acc_ref[...], gpost_ref[...])


def _ffn(x3, gpre, w_up, cw, w_down, gpost):
    b, s, d = x3.shape
    ts = FFN_ROWS
    fc = FFN_CHUNK
    r = ts // HALO
    return pl.pallas_call(
        _ffn_kernel,
        grid=(b, s // ts),
        in_specs=[pl.BlockSpec((1, ts, d), lambda bi, i: (bi, i, 0)),
                  pl.BlockSpec((1, HALO, d), lambda bi, i: (bi, jnp.maximum(i * r - 1, 0), 0)),
                  _const_spec(gpre.shape), _const_spec(w_up.shape), _const_spec(cw.shape),
                  _const_spec(w_down.shape), _const_spec(gpost.shape)],
        out_specs=pl.BlockSpec((1, ts, d), lambda bi, i: (bi, i, 0)),
        out_shape=jax.ShapeDtypeStruct((b, s, d), F32),
        scratch_shapes=[pltpu.VMEM((HALO + ts, d), BF16),
                        pltpu.VMEM((2, HALO + ts, fc), F32),
                        pltpu.VMEM((ts, fc), BF16),
                        pltpu.VMEM((ts, fc), BF16),
                        pltpu.VMEM((ts, d), F32)],
        compiler_params=_params(2),
        name="conv_ffn",
    )(x3, x3, gpre, w_up, cw, w_down, gpost)


def kernel(x, attn_pre_g, w_in, swa_sinks, swa_out_g, diff_lq1, diff_lk1, diff_lq2, diff_lk2,
           diff_subln_g, w_out, attn_post_g, ffn_pre_g, w_up, conv_w, conv_b, w_down, ffn_post_g):
    batch, seq, d = x.shape
    depth = w_in.shape[0]
    slopes = _alibi_slopes(SWA_Q_HEADS + DIFF_HEADS)
    swa_bias = jnp.asarray(_swa_bias(slopes[:SWA_Q_HEADS]))
    diff_slopes = jnp.asarray(slopes[SWA_Q_HEADS:])
    qa_cols = _swa_q_columns()
    row = lambda v: v.reshape(1, -1).astype(F32)
    qa_w = SWA_Q_HEADS * HEAD_DIM
    kv_w = SWA_KV_HEADS * HEAD_DIM
    qb_w = DIFF_HEADS * 2 * HEAD_DIM
    va0 = qa_w + kv_w
    qb0 = va0 + kv_w
    vb0 = qb0 + 2 * qb_w

    for layer in range(depth):
        lambda_init = 0.8 - 0.6 * math.exp(-0.3 * layer)
        wl = w_in[layer]
        w_main = jnp.concatenate(
            [wl[:, qa_cols], wl[:, qa_w:va0], wl[:, qb0:vb0]], axis=1).astype(BF16)
        w_vt = jnp.concatenate([wl[:, va0:qb0], wl[:, vb0:]], axis=1).T.astype(BF16)
        qa, ka, qb, kb, vat, vbt = _proj(x.reshape(batch * seq, d), row(attn_pre_g[layer]),
                                         w_main, w_vt, batch, seq)
        sink_rows = jnp.repeat(swa_sinks[layer].astype(F32) * LOG2E, WINDOW).reshape(
            SWA_KV_HEADS, 1, SWA_GROUP * WINDOW)
        ya = _swa(qa.reshape(batch, seq, -1), ka.reshape(batch, seq, -1), vat,
                  swa_bias, sink_rows, row(swa_out_g[layer]))
        yb = _diff(qb.reshape(batch, seq, -1), kb.reshape(batch, seq, -1), vbt, diff_slopes,
                   row(diff_lq1[layer]), row(diff_lk1[layer]), row(diff_lq2[layer]),
                   row(diff_lk2[layer]), row(diff_subln_g[layer]), lambda_init)
        x1 = _outproj(ya.reshape(batch * seq, -1), yb.reshape(batch * seq, -1),
                      x.reshape(batch * seq, d), w_out[layer].astype(BF16),
                      row(attn_post_g[layer]))
        cw = jnp.concatenate([conv_w[layer], conv_b[layer][None, :]], axis=0).astype(F32)
        cw = jnp.pad(cw, ((0, SUBLANES - cw.shape[0]), (0, 0)))
        x = _ffn(x1.reshape(batch, seq, d), row(ffn_pre_g[layer]), w_up[layer].astype(BF16), cw,
                 w_down[layer].astype(BF16), row(ffn_post_g[layer]))
    return x
```

```python
import functools
import math

import numpy as np
import jax
import jax.numpy as jnp
from jax import lax
from jax.experimental import pallas as pl
from jax.experimental.pallas import tpu as pltpu

F32 = jnp.float32
BF16 = jnp.bfloat16

EPS = 1e-6
HEAD_DIM = 64
SWA_Q_HEADS = 8
SWA_KV_HEADS = 2
SWA_GROUP = SWA_Q_HEADS // SWA_KV_HEADS
WINDOW = 128
DIFF_HEADS = 4
DIFF_V_DIM = 2 * HEAD_DIM
CONV_WIDTH = 3
NEG = -1e30
LOG2E = math.log2(math.e)

LANES = 128
SUBLANES = 8
VMEM_LIMIT_BYTES = 56 * 1024 * 1024

PROJ_ROWS = 512
SWA_ROWS = 512
DIFF_TK = 512
DIFF_TQ = 2 * DIFF_TK
FFN_ROWS = 512
FFN_CHUNK = 512
HALO = 16


def _alibi_slopes(n):
    def pow2(m):
        start = 2.0 ** (-8.0 / m)
        return [start ** (i + 1) for i in range(m)]
    if math.log2(n).is_integer():
        s = pow2(n)
    else:
        c = 2 ** int(math.floor(math.log2(n)))
        s = pow2(c) + pow2(2 * c)[0::2][: n - c]
    return np.array(sorted(s, reverse=True), dtype=np.float32)


def _rms(xf, g):
    return xf * lax.rsqrt(jnp.mean(xf * xf, axis=-1, keepdims=True) + EPS) * g


def _params(n_axes, flags=None):
    return pltpu.CompilerParams(
        dimension_semantics=("arbitrary",) * n_axes,
        vmem_limit_bytes=VMEM_LIMIT_BYTES,
        flags=flags)


def _const_spec(shape):
    nd = len(shape)
    return pl.BlockSpec(shape, lambda *_: (0,) * nd, pipeline_mode=pl.Buffered(1))


def _proj_kernel(x_ref, g_ref, w_ref, wvt_ref, qa_ref, ka_ref, qb_ref, kb_ref, vat_ref, vbt_ref):
    h = _rms(x_ref[...], g_ref[...]).astype(BF16)
    y = jnp.dot(h, w_ref[...], preferred_element_type=F32)
    scale = HEAD_DIM ** -0.5 * LOG2E
    o = 0
    for ref, sc in ((qa_ref, scale), (ka_ref, None), (qb_ref, scale), (kb_ref, None)):
        w = ref.shape[-1]
        piece = y[:, o:o + w]
        if sc is not None:
            piece = piece * sc
        ref[...] = piece.astype(BF16)
        o += w
    vt = lax.dot_general(wvt_ref[...], h, (((1,), (1,)), ((), ())),
                         preferred_element_type=F32).astype(BF16)
    va_rows = vat_ref.shape[1]
    vat_ref[0] = vt[:va_rows]
    vbt_ref[0, 0] = vt[va_rows:]


def _proj(x2, g, w_main, w_vt, batch, seq):
    t, d = x2.shape
    tm = PROJ_ROWS
    ns = seq // tm
    qa_w = SWA_Q_HEADS * HEAD_DIM
    ka_w = SWA_KV_HEADS * HEAD_DIM
    qb_w = DIFF_HEADS * 2 * HEAD_DIM
    vb_w = DIFF_HEADS * DIFF_V_DIM
    widths = (qa_w, ka_w, qb_w, qb_w)
    out_shape = [jax.ShapeDtypeStruct((t, w), BF16) for w in widths]
    out_shape.append(jax.ShapeDtypeStruct((batch, ka_w, seq), BF16))
    out_shape.append(jax.ShapeDtypeStruct((batch, ns, vb_w, tm), BF16))
    out_specs = [pl.BlockSpec((tm, w), lambda i: (i, 0)) for w in widths]
    out_specs.append(pl.BlockSpec((1, ka_w, tm), lambda i: (i // ns, 0, i % ns)))
    out_specs.append(pl.BlockSpec((1, 1, vb_w, tm), lambda i: (i // ns, i % ns, 0, 0)))
    return pl.pallas_call(
        _proj_kernel,
        grid=(t // tm,),
        in_specs=[pl.BlockSpec((tm, d), lambda i: (i, 0)),
                  _const_spec(g.shape), _const_spec(w_main.shape), _const_spec(w_vt.shape)],
        out_specs=out_specs,
        out_shape=out_shape,
        compiler_params=_params(1),
        name="proj_in",
    )(x2, g, w_main, w_vt)


def _swa_q_columns():
    cols = []
    for t in range(SWA_GROUP):
        for g in range(SWA_KV_HEADS):
            head = g * SWA_GROUP + t
            cols.extend(range(head * HEAD_DIM, (head + 1) * HEAD_DIM))
    return np.array(cols, dtype=np.int32)


def _swa_bias(slopes):
    qpos = np.arange(WINDOW)[None, :] + WINDOW
    kpos = np.arange(2 * WINDOW)[:, None]
    dist = (qpos - kpos).astype(np.float32)
    valid = (dist >= 0) & (dist < WINDOW)
    first_valid = valid & (kpos >= WINDOW)
    out = np.empty((2, SWA_KV_HEADS, 2 * WINDOW, SWA_GROUP * WINDOW), np.float32)
    for g in range(SWA_KV_HEADS):
        for t in range(SWA_GROUP):
            sl = np.float32(slopes[g * SWA_GROUP + t] * LOG2E)
            cols = slice(t * WINDOW, (t + 1) * WINDOW)
            out[0, g, :, cols] = np.where(valid, -sl * dist, NEG)
            out[1, g, :, cols] = np.where(first_valid, -sl * dist, NEG)
    return out


def _swa_kernel(q_ref, k_ref, kp_ref, vt_ref, vtp_ref, bias_ref, sink_ref, g_ref, o_ref):
    i = pl.program_id(1)
    nblk = q_ref.shape[1] // WINDOW
    lane = lax.broadcasted_iota(jnp.int32, (WINDOW, LANES), 1)
    for n in range(nblk):
        rows = slice(n * WINDOW, (n + 1) * WINDOW)
        if n == 0:
            kk = jnp.concatenate([kp_ref[0], k_ref[0, rows, :]], axis=0)
            vvt = jnp.concatenate([vtp_ref[0], vt_ref[0, :, rows]], axis=1)
        else:
            kk = k_ref[0, (n - 1) * WINDOW:(n + 1) * WINDOW, :]
            vvt = vt_ref[0, :, (n - 1) * WINDOW:(n + 1) * WINDOW]
        q_tiles = [q_ref[0, rows, t * LANES:(t + 1) * LANES] for t in range(SWA_GROUP)]
        pieces = []
        for g in range(SWA_KV_HEADS):
            keep = (lane < HEAD_DIM) if g == 0 else (lane >= HEAD_DIM)
            qm = jnp.concatenate([jnp.where(keep, qt, jnp.zeros_like(qt)) for qt in q_tiles],
                                 axis=0)
            s = lax.dot_general(kk, qm, (((1,), (1,)), ((), ())),
                                preferred_element_type=F32)
            if n == 0:
                bias = jnp.where(i == 0, bias_ref[1, g], bias_ref[0, g])
            else:
                bias = bias_ref[0, g]
            s = s + bias
            sink = sink_ref[g]
            m = jnp.maximum(jnp.max(s, axis=0, keepdims=True), sink)
            p = jnp.exp2(s - m)
            l = jnp.sum(p, axis=0, keepdims=True) + jnp.exp2(sink - m)
            ot = jnp.dot(vvt[g * HEAD_DIM:(g + 1) * HEAD_DIM], p.astype(BF16),
                         preferred_element_type=F32) * (1.0 / l)
            pieces.extend(ot[:, t * WINDOW:(t + 1) * WINDOW] for t in range(SWA_GROUP))
        y = jnp.concatenate(pieces, axis=0).T
        o_ref[0, rows, :] = _rms(y, g_ref[...]).astype(BF16)


def _swa(qa, ka, vat, bias, sink_rows, g):
    b, s, qw = qa.shape
    kw = ka.shape[-1]
    tq = SWA_ROWS
    r = tq // WINDOW
    cur = lambda bi, i: (bi, i, 0)
    prev = lambda bi, i: (bi, jnp.maximum(i * r - 1, 0), 0)
    cur_t = lambda bi, i: (bi, 0, i)
    prev_t = lambda bi, i: (bi, 0, jnp.maximum(i * r - 1, 0))
    return pl.pallas_call(
        _swa_kernel,
        grid=(b, s // tq),
        in_specs=[pl.BlockSpec((1, tq, qw), cur),
                  pl.BlockSpec((1, tq, kw), cur), pl.BlockSpec((1, WINDOW, kw), prev),
                  pl.BlockSpec((1, kw, tq), cur_t), pl.BlockSpec((1, kw, WINDOW), prev_t),
                  _const_spec(bias.shape), _const_spec(sink_rows.shape), _const_spec(g.shape)],
        out_specs=pl.BlockSpec((1, tq, qw), cur),
        out_shape=jax.ShapeDtypeStruct((b, s, qw), BF16),
        compiler_params=_params(2),
        name="swa_attn",
    )(qa, ka, ka, vat, vat, bias, sink_rows, g)


N_BIAS_COLS = 3
DEN_ROWS = 16
M_INIT = -1e28
DEN_LIMIT = 2.0 ** 60


def _diff_kernel(slopes_ref, q_ref, k_ref, vt_ref, lq1_ref, lk1_ref, lq2_ref, lk2_ref, g_ref,
                 o_ref, qx_ref, kf_ref, pa_ref, pb_ref, aa_ref, ab_ref, acc_ref, m_ref,
                 *, lambda_init):
    h = pl.program_id(1)
    i = pl.program_id(2)
    tq = q_ref.shape[1]
    tk = kf_ref.shape[0]
    w = q_ref.shape[2]
    slope2 = slopes_ref[h] * LOG2E
    assert tq == 2 * tk

    @pl.when(i == 0)
    def _():
        kloc = lax.broadcasted_iota(jnp.int32, (tk, w), 0).astype(F32)
        col = lax.broadcasted_iota(jnp.int32, (tk, w), 1)
        rest = slope2 * kloc
        feat = jnp.zeros((tk, w), F32)
        for n in range(N_BIAS_COLS):
            part = rest.astype(BF16).astype(F32)
            feat = jnp.where(col == n, part, feat)
            rest = rest - part
        kf_ref[...] = feat.astype(BF16)

    q = q_ref[0]
    lane = lax.broadcasted_iota(jnp.int32, q.shape, 1)
    zero = jnp.zeros_like(q)
    ones_cols = jnp.where(lane < N_BIAS_COLS, 1.0, 0.0).astype(BF16)
    qx_ref[0] = jnp.concatenate([jnp.where(lane < HEAD_DIM, q, zero), ones_cols], axis=1)
    qx_ref[1] = jnp.concatenate([jnp.where(lane >= HEAD_DIM, q, zero), ones_cols], axis=1)
    ones_rows = jnp.ones((DEN_ROWS, tk), BF16)

    def softmax(j, p_ref, alpha_ref, lo, masked, track_max):
        nq = tq - lo
        kt = k_ref[0, pl.ds(pl.multiple_of(j * tk, tk), tk), :]
        kx = jnp.concatenate([kt, kf_ref[...]], axis=1)
        c = slope2 * (j * tk).astype(F32)
        for mp in range(2):
            s = lax.dot_general(kx, qx_ref[mp, lo:tq, :], (((1,), (1,)), ((), ())),
                                preferred_element_type=F32)
            if masked:
                krow = lax.broadcasted_iota(jnp.int32, (tk, nq), 0)
                qcol = lax.broadcasted_iota(jnp.int32, (tk, nq), 1)
                s = jnp.where(krow <= qcol, s, NEG)
            m = m_ref[mp, :, lo:tq]
            if track_max:
                m_old = m
                m = jnp.maximum(m_old, jnp.max(s, axis=0, keepdims=True) + c)
                alpha_ref[mp, :, lo:tq] = jnp.exp2(m_old - m)
                m_ref[mp, :, lo:tq] = m
            p_ref[mp, :, lo:tq] = jnp.exp2(s - (m - c)).astype(BF16)

    def weighted_values(j, p_ref, alpha_ref, lo, rescale):
        vx = jnp.concatenate([vt_ref[0, j], ones_rows], axis=0)
        for mp in range(2):
            pv = jnp.dot(vx, p_ref[mp, :, lo:tq], preferred_element_type=F32)
            if rescale:
                acc_ref[mp, :, lo:tq] = alpha_ref[mp, :, lo:tq] * acc_ref[mp, :, lo:tq] + pv
            else:
                acc_ref[mp, :, lo:tq] += pv

    def attend(track_max):
        acc_ref[...] = jnp.zeros_like(acc_ref)
        last = 2 * i + 1
        softmax(last, pa_ref, aa_ref, tk, True, track_max)
        softmax(last - 1, pb_ref, ab_ref, 0, True, track_max)
        weighted_values(last, pa_ref, aa_ref, tk, track_max)

        def pair(t, carry):
            j = last - 2 - 2 * t
            softmax(j, pa_ref, aa_ref, 0, False, track_max)
            weighted_values(j + 1, pb_ref, ab_ref, 0, track_max)
            softmax(j - 1, pb_ref, ab_ref, 0, False, track_max)
            weighted_values(j, pa_ref, aa_ref, 0, track_max)
            return carry

        lax.fori_loop(0, i, pair, 0)
        weighted_values(0, pb_ref, ab_ref, 0, track_max)

    k_self = k_ref[0, pl.ds(pl.multiple_of(i * tq, tq), tq), :]
    qk = (q.astype(F32) * k_self.astype(F32)).astype(BF16)
    half = lax.broadcasted_iota(jnp.int32, (DEN_ROWS, w), 1) < HEAD_DIM
    qpos = (i * tq + lax.broadcasted_iota(jnp.int32, (1, tq), 1)).astype(F32)
    for mp in range(2):
        pick = jnp.where(half if mp == 0 else jnp.logical_not(half), 1.0, 0.0).astype(BF16)
        self_score = lax.dot_general(pick, qk, (((1,), (1,)), ((), ())),
                                     preferred_element_type=F32)[0:1]
        m_ref[mp] = self_score + slope2 * qpos
    attend(False)
    den_max = jnp.max(jnp.maximum(acc_ref[0, w:w + 1, :], acc_ref[1, w:w + 1, :]))

    @pl.when(jnp.logical_not(den_max <= DEN_LIMIT))
    def _():
        m_ref[...] = jnp.full_like(m_ref, M_INIT)
        attend(True)

    lam =(jnp.exp(jnp.sum(lq1_ref[...] * lk1_ref[...], axis=-1, keepdims=True))
           - jnp.exp(jnp.sum(lq2_ref[...] * lk2_ref[...], axis=-1, keepdims=True))
           + lambda_init)
    y = (acc_ref[0, 0:w, :] * (1.0 / acc_ref[0, w:w + 1, :])
         - lam * (acc_ref[1, 0:w, :] * (1.0 / acc_ref[1, w:w + 1, :])))
    yt = y.T
    o_ref[0] = (_rms(yt, g_ref[...]) * (1.0 - lambda_init)).astype(BF16)


def _diff(qb, kb, vbt, slopes, lq1, lk1, lq2, lk2, g, lambda_init):
    b, s, _ = qb.shape
    tq, tk = DIFF_TQ, DIFF_TK
    w = DIFF_V_DIM
    ntk = vbt.shape[1]
    grid_spec = pltpu.PrefetchScalarGridSpec(
        num_scalar_prefetch=1,
        grid=(b, DIFF_HEADS, s // tq),
        in_specs=[pl.BlockSpec((1, tq, w), lambda bi, h, i, sl: (bi, i, h)),
                  pl.BlockSpec((1, s, w), lambda bi, h, i, sl: (bi, 0, h)),
                  pl.BlockSpec((1, ntk, w, tk), lambda bi, h, i, sl: (bi, 0, h, 0)),
                  _const_spec(lq1.shape), _const_spec(lk1.shape),
                  _const_spec(lq2.shape), _const_spec(lk2.shape), _const_spec(g.shape)],
        out_specs=pl.BlockSpec((1, tq, w), lambda bi, h, i, sl: (bi, i, h)),
        scratch_shapes=[pltpu.VMEM((2, tq, 2 * w), BF16),
                        pltpu.VMEM((tk, w), BF16),
                        pltpu.VMEM((2, tk, tq), BF16),
                        pltpu.VMEM((2, tk, tq), BF16),
                        pltpu.VMEM((2, 1, tq), F32),
                        pltpu.VMEM((2, 1, tq), F32),
                        pltpu.VMEM((2, w + DEN_ROWS, tq), F32),
                        pltpu.VMEM((2, 1, tq), F32)])
    return pl.pallas_call(
        functools.partial(_diff_kernel, lambda_init=lambda_init),
        grid_spec=grid_spec,
        out_shape=jax.ShapeDtypeStruct((b, s, DIFF_HEADS * w), BF16),
        compiler_params=_params(3),
        name="diff_attn",
    )(slopes, qb, kb, vbt, lq1, lk1, lq2, lk2, g)


def _outproj_kernel(ya_ref, yb_ref, x_ref, w_ref, g_ref, o_ref):
    ka = ya_ref.shape[-1]
    o = jnp.dot(ya_ref[...], w_ref[0:ka, :], preferred_element_type=F32)
    o = o + jnp.dot(yb_ref[...], w_ref[ka:, :], preferred_element_type=F32)
    o_ref[...] = x_ref[...] + _rms(o, g_ref[...])


def _outproj(ya, yb, x2, w_out, g):
    t, d = x2.shape
    tm = PROJ_ROWS
    return pl.pallas_call(
        _outproj_kernel,
        grid=(t // tm,),
        in_specs=[pl.BlockSpec((tm, ya.shape[-1]), lambda i: (i, 0)),
                  pl.BlockSpec((tm, yb.shape[-1]), lambda i: (i, 0)),
                  pl.BlockSpec((tm, d), lambda i: (i, 0)),
                  _const_spec(w_out.shape), _const_spec(g.shape)],
        out_specs=pl.BlockSpec((tm, d), lambda i: (i, 0)),
        out_shape=jax.ShapeDtypeStruct((t, d), F32),
        compiler_params=_params(1),
        name="proj_out",
    )(ya, yb, x2, w_out, g)


def _gelu_tanh(x):
    cdf = 0.5 * (1.0 + jnp.tanh(math.sqrt(2.0 / math.pi) * (x + 0.044715 * (x * x * x))))
    return x * cdf


def _ffn_kernel(x_ref, halo_ref, gpre_ref, wup_ref, cw_ref, wd_ref, gpost_ref,
                o_ref, h_ref, u_ref, acta_ref, actb_ref, acc_ref):
    i = pl.program_id(1)
    ts = x_ref.shape[1]
    d_ff = wd_ref.shape[0]
    fc = acta_ref.shape[-1]
    nlt = fc // LANES
    nchunk = d_ff // fc
    assert nchunk % 2 == 0

    gpre = gpre_ref[...]
    h_ref[0:HALO, :] = jnp.where(i == 0, 0.0, _rms(halo_ref[0], gpre)).astype(BF16)
    h_ref[HALO:, :] = _rms(x_ref[0], gpre).astype(BF16)

    def cols(c, half):
        return pl.ds(pl.multiple_of(half * d_ff + c * fc, fc), fc)

    def up_act(c, act_ref):
        hh = h_ref[...]
        outs = []
        for half in range(2):
            u = jnp.dot(hh, wup_ref[:, cols(c, half)], preferred_element_type=F32)
            for lt in range(nlt):
                u_ref[half * nlt + lt] = u[:, lt * LANES:(lt + 1) * LANES]
            cp = cw_ref[:, cols(c, half)]
            out = cp[CONV_WIDTH:CONV_WIDTH + 1]
            for tap in range(CONV_WIDTH):
                start = HALO - (CONV_WIDTH - 1) + tap
                shifted = jnp.concatenate(
                    [u_ref[pl.ds(half * nlt + lt, 1, stride=2), pl.ds(start, ts), :].reshape(ts, LANES)
                     for lt in range(nlt)], axis=1)
                out = out + cp[tap:tap + 1] * shifted
            outs.append(out)
        act_ref[...] = (_gelu_tanh(outs[0]) * outs[1]).astype(BF16)

    def down(c, act_ref):
        rows = pl.ds(pl.multiple_of(c * fc, fc), fc)
        acc_ref[...] += jnp.dot(act_ref[...], wd_ref[rows, :], preferred_element_type=F32)

    acc_ref[...] = jnp.zeros_like(acc_ref)
    up_act(0, acta_ref)

    def body(t, carry):
        c = 2 * t + 1
        up_act(c, actb_ref)
        down(c - 1, acta_ref)
        up_act(c + 1, acta_ref)
        down(c, actb_ref)
        return carry

    lax.fori_loop(0, (nchunk - 2) // 2, body, 0)
    up_act(nchunk - 1, actb_ref)
    down(nchunk - 2, acta_ref)
    down(nchunk - 1, actb_ref)

    o_ref[0] = x_ref[0] + _rms(acc_ref[...], gpost_ref[...])


def _ffn(x3, gpre, w_up, cw, w_down, gpost):
    b, s, d = x3.shape
    ts = FFN_ROWS
    fc = FFN_CHUNK
    r = ts // HALO
    return pl.pallas_call(
        _ffn_kernel,
        grid=(b, s // ts),
        in_specs=[pl.BlockSpec((1, ts, d), lambda bi, i: (bi, i, 0)),
                  pl.BlockSpec((1, HALO, d), lambda bi, i: (bi, jnp.maximum(i * r - 1, 0), 0)),
                  _const_spec(gpre.shape), _const_spec(w_up.shape), _const_spec(cw.shape),
                  _const_spec(w_down.shape), _const_spec(gpost.shape)],
        out_specs=pl.BlockSpec((1, ts, d), lambda bi, i: (bi, i, 0)),
        out_shape=jax.ShapeDtypeStruct((b, s, d), F32),
        scratch_shapes=[pltpu.VMEM((HALO + ts, d), BF16),
                        pltpu.VMEM((2 * fc // LANES, HALO + ts, LANES), F32),
                        pltpu.VMEM((ts, fc), BF16),
                        pltpu.VMEM((ts, fc), BF16),
                        pltpu.VMEM((ts, d), F32)],
        compiler_params=_params(2),
        name="conv_ffn",
    )(x3, x3, gpre, w_up, cw, w_down, gpost)


def kernel(x, attn_pre_g, w_in, swa_sinks, swa_out_g, diff_lq1, diff_lk1, diff_lq2, diff_lk2,
           diff_subln_g, w_out, attn_post_g, ffn_pre_g, w_up, conv_w, conv_b, w_down, ffn_post_g):
    batch, seq, d = x.shape
    depth = w_in.shape[0]
    slopes = _alibi_slopes(SWA_Q_HEADS + DIFF_HEADS)
    swa_bias = jnp.asarray(_swa_bias(slopes[:SWA_Q_HEADS]))
    diff_slopes = jnp.asarray(slopes[SWA_Q_HEADS:])
    qa_cols = _swa_q_columns()
    row = lambda v: v.reshape(1, -1).astype(F32)
    qa_w = SWA_Q_HEADS * HEAD_DIM
    kv_w = SWA_KV_HEADS * HEAD_DIM
    qb_w = DIFF_HEADS * 2 * HEAD_DIM
    va0 = qa_w + kv_w
    qb0 = va0 + kv_w
    vb0 = qb0 + 2 * qb_w

    for layer in range(depth):
        lambda_init = 0.8 - 0.6 * math.exp(-0.3 * layer)
        wl = w_in[layer]
        w_main = jnp.concatenate(
            [wl[:, qa_cols], wl[:, qa_w:va0], wl[:, qb0:vb0]], axis=1).astype(BF16)
        w_vt = jnp.concatenate([wl[:, va0:qb0], wl[:, vb0:]], axis=1).T.astype(BF16)
        qa, ka, qb, kb, vat, vbt = _proj(x.reshape(batch * seq, d), row(attn_pre_g[layer]),
                                         w_main, w_vt, batch, seq)
        sink_rows = jnp.repeat(swa_sinks[layer].astype(F32) * LOG2E, WINDOW).reshape(
            SWA_KV_HEADS, 1, SWA_GROUP * WINDOW)
        ya = _swa(qa.reshape(batch, seq, -1), ka.reshape(batch, seq, -1), vat,
                  swa_bias, sink_rows, row(swa_out_g[layer]))
        yb = _diff(qb.reshape(batch, seq, -1), kb.reshape(batch, seq, -1), vbt, diff_slopes,
                   row(diff_lq1[layer]), row(diff_lk1[layer]), row(diff_lq2[layer]),
                   row(diff_lk2[layer]), row(diff_subln_g[layer]), lambda_init)
        x1 = _outproj(ya.reshape(batch * seq, -1), yb.reshape(batch * seq, -1),
                      x.reshape(batch * seq, d), w_out[layer].astype(BF16),
                      row(attn_post_g[layer]))
        cw = jnp.concatenate([conv_w[layer], conv_b[layer][None, :]], axis=0).astype(F32)
        cw = jnp.pad(cw, ((0, SUBLANES - cw.shape[0]), (0, 0)))
        x = _ffn(x1.reshape(batch, seq, d), row(ffn_pre_g[layer]), w_up[layer].astype(BF16), cw,
                 w_down[layer].astype(BF16), row(ffn_post_g[layer]))
    return x
```

```python
import functools
import math

import numpy as np
import jax
import jax.numpy as jnp
from jax import lax
from jax.experimental import pallas as pl
from jax.experimental.pallas import tpu as pltpu

F32 = jnp.float32
BF16 = jnp.bfloat16

EPS = 1e-6
HEAD_DIM = 64
SWA_Q_HEADS = 8
SWA_KV_HEADS = 2
SWA_GROUP = SWA_Q_HEADS // SWA_KV_HEADS
WINDOW = 128
DIFF_HEADS = 4
DIFF_V_DIM = 2 * HEAD_DIM
CONV_WIDTH = 3
NEG = -1e30
LOG2E = math.log2(math.e)

LANES = 128
SUBLANES = 8
DEN_ROWS = 16
VMEM_LIMIT_BYTES = 56 * 1024 * 1024

PROJ_ROWS = 512
SWA_ROWS = 512
DIFF_TK = 512
DIFF_TQ = 4 * DIFF_TK
FFN_ROWS = 512
FFN_CHUNK = 512
HALO = 16


def _alibi_slopes(n):
    def pow2(m):
        start = 2.0 ** (-8.0 / m)
        return [start ** (i + 1) for i in range(m)]
    if math.log2(n).is_integer():
        s = pow2(n)
    else:
        c = 2 ** int(math.floor(math.log2(n)))
        s = pow2(c) + pow2(2 * c)[0::2][: n - c]
    return np.array(sorted(s, reverse=True), dtype=np.float32)


def _rms(xf, g):
    return xf * lax.rsqrt(jnp.mean(xf * xf, axis=-1, keepdims=True) + EPS) * g


def _params(n_axes, flags=None):
    return pltpu.CompilerParams(
        dimension_semantics=("arbitrary",) * n_axes,
        vmem_limit_bytes=VMEM_LIMIT_BYTES,
        flags=flags)


def _const_spec(shape):
    nd = len(shape)
    return pl.BlockSpec(shape, lambda *_: (0,) * nd, pipeline_mode=pl.Buffered(1))


def _proj_kernel(x_ref, g_ref, w_ref, wvt_ref, qa_ref, ka_ref, qb_ref, kb_ref, vat_ref, vbt_ref):
    h = _rms(x_ref[...], g_ref[...]).astype(BF16)
    y = jnp.dot(h, w_ref[...], preferred_element_type=F32)
    scale = HEAD_DIM ** -0.5 * LOG2E
    o = 0
    for ref, sc in ((qa_ref, scale), (ka_ref, None), (qb_ref, scale), (kb_ref, None)):
        w = ref.shape[-1]
        piece = y[:, o:o + w]
        if sc is not None:
            piece = piece * sc
        ref[...] = piece.astype(BF16)
        o += w
    vt = lax.dot_general(wvt_ref[...], h, (((1,), (1,)), ((), ())),
                         preferred_element_type=F32).astype(BF16)
    va_rows = vat_ref.shape[1]
    vat_ref[0] = vt[:va_rows]
    vbt_ref[0, 0] = vt[va_rows:]


def _proj(x2, g, w_main, w_vt, batch, seq):
    t, d = x2.shape
    tm = PROJ_ROWS
    ns = seq // tm
    qa_w = SWA_Q_HEADS * HEAD_DIM
    ka_w = SWA_KV_HEADS * HEAD_DIM
    qb_w = DIFF_HEADS * 2 * HEAD_DIM
    vb_w = DIFF_HEADS * DIFF_V_DIM
    widths = (qa_w, ka_w, qb_w, qb_w)
    out_shape = [jax.ShapeDtypeStruct((t, w), BF16) for w in widths]
    out_shape.append(jax.ShapeDtypeStruct((batch, ka_w, seq), BF16))
    out_shape.append(jax.ShapeDtypeStruct((batch, ns, vb_w, tm), BF16))
    out_specs = [pl.BlockSpec((tm, w), lambda i: (i, 0)) for w in widths]
    out_specs.append(pl.BlockSpec((1, ka_w, tm), lambda i: (i // ns, 0, i % ns)))
    out_specs.append(pl.BlockSpec((1, 1, vb_w, tm), lambda i: (i // ns, i % ns, 0, 0)))
    return pl.pallas_call(
        _proj_kernel,
        grid=(t // tm,),
        in_specs=[pl.BlockSpec((tm, d), lambda i: (i, 0)),
                  _const_spec(g.shape), _const_spec(w_main.shape), _const_spec(w_vt.shape)],
        out_specs=out_specs,
        out_shape=out_shape,
        compiler_params=_params(1),
        name="proj_in",
    )(x2, g, w_main, w_vt)


def _swa_q_columns():
    cols = []
    for t in range(SWA_GROUP):
        for g in range(SWA_KV_HEADS):
            head = g * SWA_GROUP + t
            cols.extend(range(head * HEAD_DIM, (head + 1) * HEAD_DIM))
    return np.array(cols, dtype=np.int32)


def _swa_bias(slopes):
    qpos = np.arange(WINDOW)[None, :] + WINDOW
    kpos = np.arange(2 * WINDOW)[:, None]
    dist = (qpos - kpos).astype(np.float32)
    valid = (dist >= 0) & (dist < WINDOW)
    first_valid = valid & (kpos >= WINDOW)
    out = np.empty((2, SWA_KV_HEADS, 2 * WINDOW, SWA_GROUP * WINDOW), np.float32)
    for g in range(SWA_KV_HEADS):
        for t in range(SWA_GROUP):
            sl = np.float32(slopes[g * SWA_GROUP + t] * LOG2E)
            cols = slice(t * WINDOW, (t + 1) * WINDOW)
            out[0, g, :, cols] = np.where(valid, -sl * dist, NEG)
            out[1, g, :, cols] = np.where(first_valid, -sl * dist, NEG)
    return out


def _swa_kernel(q_ref, k_ref, kp_ref, vt_ref, vtp_ref, bias_ref, sink_ref, g_ref, o_ref):
    i = pl.program_id(1)
    nblk = q_ref.shape[1] // WINDOW
    lane = lax.broadcasted_iota(jnp.int32, (WINDOW, LANES), 1)
    ones_rows = jnp.ones((DEN_ROWS, 2 * WINDOW), BF16)

    def scores(n, g):
        rows = slice(n * WINDOW, (n + 1) * WINDOW)
        if n == 0:
            kk = jnp.concatenate([kp_ref[0], k_ref[0, rows, :]], axis=0)
        else:
            kk = k_ref[0, (n - 1) * WINDOW:(n + 1) * WINDOW, :]
        keep = (lane < HEAD_DIM) if g == 0 else (lane >= HEAD_DIM)
        q_tiles = [q_ref[0, rows, t * LANES:(t + 1) * LANES] for t in range(SWA_GROUP)]
        qm = jnp.concatenate([jnp.where(keep, qt, jnp.zeros_like(qt)) for qt in q_tiles],
                             axis=0)
        s = lax.dot_general(kk, qm, (((1,), (1,)), ((), ())),
                            preferred_element_type=F32)
        if n == 0:
            bias = jnp.where(i == 0, bias_ref[1, g], bias_ref[0, g])
        else:
            bias = bias_ref[0, g]
        return s + bias

    def weighted_values(n, g, s):
        if n == 0:
            vvt = jnp.concatenate([vtp_ref[0], vt_ref[0, :, 0:WINDOW]], axis=1)
        else:
            vvt = vt_ref[0, :, (n - 1) * WINDOW:(n + 1) * WINDOW]
        sink = sink_ref[g]
        m = jnp.maximum(jnp.max(s, axis=0, keepdims=True), sink)
        p = jnp.exp2(s - m).astype(BF16)
        vx = jnp.concatenate([vvt[g * HEAD_DIM:(g + 1) * HEAD_DIM], ones_rows], axis=0)
        o = jnp.dot(vx, p, preferred_element_type=F32)
        l = o[HEAD_DIM:HEAD_DIM + 1] + jnp.exp2(sink - m)
        ot = o[0:HEAD_DIM] * (1.0 / l)
        return [ot[:, t * WINDOW:(t + 1) * WINDOW] for t in range(SWA_GROUP)]

    tasks = [(n, g) for n in range(nblk) for g in range(SWA_KV_HEADS)]
    s_next = scores(*tasks[0])
    pieces = []
    for idx, (n, g) in enumerate(tasks):
        s_cur = s_next
        if idx + 1 < len(tasks):
            s_next = scores(*tasks[idx + 1])
        pieces.extend(weighted_values(n, g, s_cur))
        if g == SWA_KV_HEADS - 1:
            y = jnp.concatenate(pieces, axis=0).T
            o_ref[0, n * WINDOW:(n + 1) * WINDOW, :] = _rms(y, g_ref[...]).astype(BF16)
            pieces = []


def _swa(qa, ka, vat, bias, sink_rows, g):
    b, s, qw = qa.shape
    kw = ka.shape[-1]
    tq = SWA_ROWS
    r = tq // WINDOW
    cur = lambda bi, i: (bi, i, 0)
    prev = lambda bi, i: (bi, jnp.maximum(i * r - 1, 0), 0)
    cur_t = lambda bi, i: (bi, 0, i)
    prev_t = lambda bi, i: (bi, 0, jnp.maximum(i * r - 1, 0))
    return pl.pallas_call(
        _swa_kernel,
        grid=(b, s // tq),
        in_specs=[pl.BlockSpec((1, tq, qw), cur),
                  pl.BlockSpec((1, tq, kw), cur), pl.BlockSpec((1, WINDOW, kw), prev),
                  pl.BlockSpec((1, kw, tq), cur_t), pl.BlockSpec((1, kw, WINDOW), prev_t),
                  _const_spec(bias.shape), _const_spec(sink_rows.shape), _const_spec(g.shape)],
        out_specs=pl.BlockSpec((1, tq, qw), cur),
        out_shape=jax.ShapeDtypeStruct((b, s, qw), BF16),
        compiler_params=_params(2),
        name="swa_attn",
    )(qa, ka, ka, vat, vat, bias, sink_rows, g)


N_BIAS_COLS = 3
M_INIT = -1e28
DEN_LIMIT = 2.0 ** 60


def _diff_kernel(slopes_ref, q_ref, k_ref, vt_ref, lq1_ref, lk1_ref, lq2_ref, lk2_ref, g_ref,
                 o_ref, qx_ref, kf_ref, pa_ref, pb_ref, aa_ref, ab_ref, acc_ref, m_ref,
                 *, lambda_init):
    h = pl.program_id(1)
    i = pl.program_id(2)
    tq = q_ref.shape[1]
    tk = kf_ref.shape[0]
    w = q_ref.shape[2]
    slope2 = slopes_ref[h] * LOG2E
    ndiag = tq // tk
    assert tq == ndiag * tk and ndiag % 2 == 0

    @pl.when(i == 0)
    def _():
        kloc = lax.broadcasted_iota(jnp.int32, (tk, w), 0).astype(F32)
        col = lax.broadcasted_iota(jnp.int32, (tk, w), 1)
        rest = slope2 * kloc
        feat = jnp.zeros((tk, w), F32)
        for n in range(N_BIAS_COLS):
            part = rest.astype(BF16).astype(F32)
            feat = jnp.where(col == n, part, feat)
            rest = rest - part
        kf_ref[...] = feat.astype(BF16)

    q = q_ref[0]
    lane = lax.broadcasted_iota(jnp.int32, q.shape, 1)
    zero = jnp.zeros_like(q)
    ones_cols = jnp.where(lane < N_BIAS_COLS, 1.0, 0.0).astype(BF16)
    qx_ref[0] = jnp.concatenate([jnp.where(lane < HEAD_DIM, q, zero), ones_cols], axis=1)
    qx_ref[1] = jnp.concatenate([jnp.where(lane >= HEAD_DIM, q, zero), ones_cols], axis=1)
    ones_rows = jnp.ones((DEN_ROWS, tk), BF16)

    def softmax(j, p_ref, alpha_ref, lo, masked, track_max):
        nq = tq - lo
        kt = k_ref[0, pl.ds(pl.multiple_of(j * tk, tk), tk), :]
        kx = jnp.concatenate([kt, kf_ref[...]], axis=1)
        c = slope2 * (j * tk).astype(F32)
        for mp in range(2):
            s = lax.dot_general(kx, qx_ref[mp, lo:tq, :], (((1,), (1,)), ((), ())),
                                preferred_element_type=F32)
            if masked:
                krow = lax.broadcasted_iota(jnp.int32, (tk, nq), 0)
                qcol = lax.broadcasted_iota(jnp.int32, (tk, nq), 1)
                s = jnp.where(krow <= qcol, s, NEG)
            m = m_ref[mp, :, lo:tq]
            if track_max:
                m_old = m
                m = jnp.maximum(m_old, jnp.max(s, axis=0, keepdims=True) + c)
                alpha_ref[mp, :, lo:tq] = jnp.exp2(m_old - m)
                m_ref[mp, :, lo:tq] = m
            p_ref[mp, :, lo:tq] = jnp.exp2(s - (m - c)).astype(BF16)

    def weighted_values(j, p_ref, alpha_ref, lo, rescale):
        vx = jnp.concatenate([vt_ref[0, j], ones_rows], axis=0)
        for mp in range(2):
            pv = jnp.dot(vx, p_ref[mp, :, lo:tq], preferred_element_type=F32)
            if rescale:
                acc_ref[mp, :, lo:tq] = alpha_ref[mp, :, lo:tq] * acc_ref[mp, :, lo:tq] + pv
            else:
                acc_ref[mp, :, lo:tq] += pv

    def attend(track_max):
        acc_ref[...] = jnp.zeros_like(acc_ref)
        bufs = ((pa_ref, aa_ref), (pb_ref, ab_ref))
        first = ndiag * i
        for step, d in enumerate(range(ndiag - 1, -1, -1)):
            softmax(first + d, *bufs[step % 2], d * tk, True, track_max)
            if step > 0:
                weighted_values(first + d + 1, *bufs[(step - 1) % 2], (d + 1) * tk, track_max)

        def pair(t, carry):
            j = first - 1 - 2 * t
            softmax(j, pa_ref, aa_ref, 0, False, track_max)
            weighted_values(j + 1, pb_ref, ab_ref, 0, track_max)
            softmax(j - 1, pb_ref, ab_ref, 0, False, track_max)
            weighted_values(j, pa_ref, aa_ref, 0, track_max)
            return carry

        lax.fori_loop(0, first // 2, pair, 0)
        weighted_values(0, pb_ref, ab_ref, 0, track_max)

    k_self = k_ref[0, pl.ds(pl.multiple_of(i * tq, tq), tq), :]
    qk = (q.astype(F32) * k_self.astype(F32)).astype(BF16)
    half = lax.broadcasted_iota(jnp.int32, (DEN_ROWS, w), 1) < HEAD_DIM
    qpos = (i * tq + lax.broadcasted_iota(jnp.int32, (1, tq), 1)).astype(F32)
    for mp in range(2):
        pick = jnp.where(half if mp == 0 else jnp.logical_not(half), 1.0, 0.0).astype(BF16)
        self_score = lax.dot_general(pick, qk, (((1,), (1,)), ((), ())),
                                     preferred_element_type=F32)[0:1]
        m_ref[mp] = self_score + slope2 * qpos
    attend(False)
    den_max = jnp.max(jnp.maximum(acc_ref[0, w:w + 1, :], acc_ref[1, w:w + 1, :]))

    @pl.when(jnp.logical_not(den_max <= DEN_LIMIT))
    def _():
        m_ref[...] = jnp.full_like(m_ref, M_INIT)
        attend(True)

    lam =(jnp.exp(jnp.sum(lq1_ref[...] * lk1_ref[...], axis=-1, keepdims=True))
           - jnp.exp(jnp.sum(lq2_ref[...] * lk2_ref[...], axis=-1, keepdims=True))
           + lambda_init)
    y = (acc_ref[0, 0:w, :] * (1.0 / acc_ref[0, w:w + 1, :])
         - lam * (acc_ref[1, 0:w, :] * (1.0 / acc_ref[1, w:w + 1, :])))
    yt = y.T
    o_ref[0] = (_rms(yt, g_ref[...]) * (1.0 - lambda_init)).astype(BF16)


def _diff(qb, kb, vbt, slopes, lq1, lk1, lq2, lk2, g, lambda_init):
    b, s, _ = qb.shape
    tq, tk = DIFF_TQ, DIFF_TK
    w = DIFF_V_DIM
    ntk = vbt.shape[1]
    grid_spec = pltpu.PrefetchScalarGridSpec(
        num_scalar_prefetch=1,
        grid=(b, DIFF_HEADS, s // tq),
        in_specs=[pl.BlockSpec((1, tq, w), lambda bi, h, i, sl: (bi, i, h)),
                  pl.BlockSpec((1, s, w), lambda bi, h, i, sl: (bi, 0, h)),
                  pl.BlockSpec((1, ntk, w, tk), lambda bi, h, i, sl: (bi, 0, h, 0)),
                  _const_spec(lq1.shape), _const_spec(lk1.shape),
                  _const_spec(lq2.shape), _const_spec(lk2.shape), _const_spec(g.shape)],
        out_specs=pl.BlockSpec((1, tq, w), lambda bi, h, i, sl: (bi, i, h)),
        scratch_shapes=[pltpu.VMEM((2, tq, 2 * w), BF16),
                        pltpu.VMEM((tk, w), BF16),
                        pltpu.VMEM((2, tk, tq), BF16),
                        pltpu.VMEM((2, tk, tq), BF16),
                        pltpu.VMEM((2, 1, tq), F32),
                        pltpu.VMEM((2, 1, tq), F32),
                        pltpu.VMEM((2, w + DEN_ROWS, tq), F32),
                        pltpu.VMEM((2, 1, tq), F32)])
    return pl.pallas_call(
        functools.partial(_diff_kernel, lambda_init=lambda_init),
        grid_spec=grid_spec,
        out_shape=jax.ShapeDtypeStruct((b, s, DIFF_HEADS * w), BF16),
        compiler_params=_params(3),
        name="diff_attn",
    )(slopes, qb, kb, vbt, lq1, lk1, lq2, lk2, g)


def _outproj_kernel(ya_ref, yb_ref, x_ref, w_ref, g_ref, o_ref):
    ka = ya_ref.shape[-1]
    o = jnp.dot(ya_ref[...], w_ref[0:ka, :], preferred_element_type=F32)
    o = o + jnp.dot(yb_ref[...], w_ref[ka:, :], preferred_element_type=F32)
    o_ref[...] = x_ref[...] + _rms(o, g_ref[...])


def _outproj(ya, yb, x2, w_out, g):
    t, d = x2.shape
    tm = PROJ_ROWS
    return pl.pallas_call(
        _outproj_kernel,
        grid=(t // tm,),
        in_specs=[pl.BlockSpec((tm, ya.shape[-1]), lambda i: (i, 0)),
                  pl.BlockSpec((tm, yb.shape[-1]), lambda i: (i, 0)),
                  pl.BlockSpec((tm, d), lambda i: (i, 0)),
                  _const_spec(w_out.shape), _const_spec(g.shape)],
        out_specs=pl.BlockSpec((tm, d), lambda i: (i, 0)),
        out_shape=jax.ShapeDtypeStruct((t, d), F32),
        compiler_params=_params(1),
        name="proj_out",
    )(ya, yb, x2, w_out, g)


def _gelu_tanh(x):
    cdf = 0.5 * (1.0 + jnp.tanh(math.sqrt(2.0 / math.pi) * (x + 0.044715 * (x * x * x))))
    return x * cdf


def _ffn_kernel(x_ref, halo_ref, gpre_ref, wup_ref, cw_ref, wd_ref, gpost_ref,
                o_ref, h_ref, u_ref, acta_ref, actb_ref, acc_ref):
    i = pl.program_id(1)
    ts = x_ref.shape[1]
    d_ff = wd_ref.shape[0]
    fc = acta_ref.shape[-1]
    nlt = fc // LANES
    nchunk = d_ff // fc
    assert nchunk % 2 == 0

    gpre = gpre_ref[...]
    h_ref[0:HALO, :] = jnp.where(i == 0, 0.0, _rms(halo_ref[0], gpre)).astype(BF16)
    h_ref[HALO:, :] = _rms(x_ref[0], gpre).astype(BF16)

    def cols(c, half):
        return pl.ds(pl.multiple_of(half * d_ff + c * fc, fc), fc)

    def up_act(c, act_ref):
        hh = h_ref[...]
        outs = []
        for half in range(2):
            u = jnp.dot(hh, wup_ref[:, cols(c, half)], preferred_element_type=F32)
            for lt in range(nlt):
                u_ref[half * nlt + lt] = u[:, lt * LANES:(lt + 1) * LANES]
            cp = cw_ref[:, cols(c, half)]
            out = cp[CONV_WIDTH:CONV_WIDTH + 1]
            for tap in range(CONV_WIDTH):
                start = HALO - (CONV_WIDTH - 1) + tap
                shifted = jnp.concatenate(
                    [u_ref[pl.ds(half * nlt + lt, 1, stride=2), pl.ds(start, ts), :].reshape(ts, LANES)
                     for lt in range(nlt)], axis=1)
                out = out + cp[tap:tap + 1] * shifted
            outs.append(out)
        act_ref[...] = (_gelu_tanh(outs[0]) * outs[1]).astype(BF16)

    def down(c, act_ref):
        rows = pl.ds(pl.multiple_of(c * fc, fc), fc)
        acc_ref[...] += jnp.dot(act_ref[...], wd_ref[rows, :], preferred_element_type=F32)

    acc_ref[...] = jnp.zeros_like(acc_ref)
    up_act(0, acta_ref)

    def body(t, carry):
        c = 2 * t + 1
        up_act(c, actb_ref)
        down(c - 1, acta_ref)
        up_act(c + 1, acta_ref)
        down(c, actb_ref)
        return carry

    lax.fori_loop(0, (nchunk - 2) // 2, body, 0)
    up_act(nchunk - 1, actb_ref)
    down(nchunk - 2, acta_ref)
    down(nchunk - 1, actb_ref)

    o_ref[0] = x_ref[0] + _rms(acc_ref[...], gpost_ref[...])


def _ffn(x3, gpre, w_up, cw, w_down, gpost):
    b, s, d = x3.shape
    ts = FFN_ROWS
    fc = FFN_CHUNK
    r = ts // HALO
    return pl.pallas_call(
        _ffn_kernel,
        grid=(b, s // ts),
        in_specs=[pl.BlockSpec((1, ts, d), lambda bi, i: (bi, i, 0)),
                  pl.BlockSpec((1, HALO, d), lambda bi, i: (bi, jnp.maximum(i * r - 1, 0), 0)),
                  _const_spec(gpre.shape), _const_spec(w_up.shape), _const_spec(cw.shape),
                  _const_spec(w_down.shape), _const_spec(gpost.shape)],
        out_specs=pl.BlockSpec((1, ts, d), lambda bi, i: (bi, i, 0)),
        out_shape=jax.ShapeDtypeStruct((b, s, d), F32),
        scratch_shapes=[pltpu.VMEM((HALO + ts, d), BF16),
                        pltpu.VMEM((2 * fc // LANES, HALO + ts, LANES), F32),
                        pltpu.VMEM((ts, fc), BF16),
                        pltpu.VMEM((ts, fc), BF16),
                        pltpu.VMEM((ts, d), F32)],
        compiler_params=_params(2),
        name="conv_ffn",
    )(x3, x3, gpre, w_up, cw, w_down, gpost)


def kernel(x, attn_pre_g, w_in, swa_sinks, swa_out_g, diff_lq1, diff_lk1, diff_lq2, diff_lk2,
           diff_subln_g, w_out, attn_post_g, ffn_pre_g, w_up, conv_w, conv_b, w_down, ffn_post_g):
    batch, seq, d = x.shape
    depth = w_in.shape[0]
    slopes = _alibi_slopes(SWA_Q_HEADS + DIFF_HEADS)
    swa_bias = jnp.asarray(_swa_bias(slopes[:SWA_Q_HEADS]))
    diff_slopes = jnp.asarray(slopes[SWA_Q_HEADS:])
    qa_cols = _swa_q_columns()
    row = lambda v: v.reshape(1, -1).astype(F32)
    qa_w = SWA_Q_HEADS * HEAD_DIM
    kv_w = SWA_KV_HEADS * HEAD_DIM
    qb_w = DIFF_HEADS * 2 * HEAD_DIM
    va0 = qa_w + kv_w
    qb0 = va0 + kv_w
    vb0 = qb0 + 2 * qb_w

    for layer in range(depth):
        lambda_init = 0.8 - 0.6 * math.exp(-0.3 * layer)
        wl = w_in[layer]
        w_main = jnp.concatenate(
            [wl[:, qa_cols], wl[:, qa_w:va0], wl[:, qb0:vb0]], axis=1).astype(BF16)
        w_vt = jnp.concatenate([wl[:, va0:qb0], wl[:, vb0:]], axis=1).T.astype(BF16)
        qa, ka, qb, kb, vat, vbt = _proj(x.reshape(batch * seq, d), row(attn_pre_g[layer]),
                                         w_main, w_vt, batch, seq)
        sink_rows = jnp.repeat(swa_sinks[layer].astype(F32) * LOG2E, WINDOW).reshape(
            SWA_KV_HEADS, 1, SWA_GROUP * WINDOW)
        ya = _swa(qa.reshape(batch, seq, -1), ka.reshape(batch, seq, -1), vat,
                  swa_bias, sink_rows, row(swa_out_g[layer]))
        yb = _diff(qb.reshape(batch, seq, -1), kb.reshape(batch, seq, -1), vbt, diff_slopes,
                   row(diff_lq1[layer]), row(diff_lk1[layer]), row(diff_lq2[layer]),
                   row(diff_lk2[layer]), row(diff_subln_g[layer]), lambda_init)
        x1 = _outproj(ya.reshape(batch * seq, -1), yb.reshape(batch * seq, -1),
                      x.reshape(batch * seq, d), w_out[layer].astype(BF16),
                      row(attn_post_g[layer]))
        cw = jnp.concatenate([conv_w[layer], conv_b[layer][None, :]], axis=0).astype(F32)
        cw = jnp.pad(cw, ((0, SUBLANES - cw.shape[0]), (0, 0)))
        x = _ffn(x1.reshape(batch, seq, d), row(ffn_pre_g[layer]), w_up[layer].astype(BF16), cw,
                 w_down[layer].astype(BF16), row(ffn_post_g[layer]))
    return x
```

```python
import functools
import math

import numpy as np
import jax
import jax.numpy as jnp
from jax import lax
from jax.experimental import pallas as pl
from jax.experimental.pallas import tpu as pltpu

F32 = jnp.float32
BF16 = jnp.bfloat16

EPS = 1e-6
HEAD_DIM = 64
SWA_Q_HEADS = 8
SWA_KV_HEADS = 2
SWA_GROUP = SWA_Q_HEADS // SWA_KV_HEADS
WINDOW = 128
DIFF_HEADS = 4
DIFF_V_DIM = 2 * HEAD_DIM
CONV_WIDTH = 3
NEG = -1e30
LOG2E = math.log2(math.e)

LANES = 128
SUBLANES = 8
DEN_ROWS = 16
VMEM_LIMIT_BYTES = 56 * 1024 * 1024

PROJ_ROWS = 512
SWA_ROWS = 512
DIFF_TK = 512
DIFF_TQ = 4 * DIFF_TK
FFN_ROWS = 512
FFN_CHUNK = 512
HALO = SUBLANES


def _alibi_slopes(n):
    def pow2(m):
        start = 2.0 ** (-8.0 / m)
        return [start ** (i + 1) for i in range(m)]
    if math.log2(n).is_integer():
        s = pow2(n)
    else:
        c = 2 ** int(math.floor(math.log2(n)))
        s = pow2(c) + pow2(2 * c)[0::2][: n - c]
    return np.array(sorted(s, reverse=True), dtype=np.float32)


def _rms(xf, g):
    return xf * lax.rsqrt(jnp.mean(xf * xf, axis=-1, keepdims=True) + EPS) * g


def _params(n_axes, flags=None):
    return pltpu.CompilerParams(
        dimension_semantics=("arbitrary",) * n_axes,
        vmem_limit_bytes=VMEM_LIMIT_BYTES,
        flags=flags)


def _const_spec(shape):
    nd = len(shape)
    return pl.BlockSpec(shape, lambda *_: (0,) * nd, pipeline_mode=pl.Buffered(1))


def _proj_kernel(x_ref, g_ref, w_ref, wvt_ref, qa_ref, ka_ref, qb_ref, kb_ref, vat_ref, vbt_ref):
    h = _rms(x_ref[...], g_ref[...]).astype(BF16)
    y = jnp.dot(h, w_ref[...], preferred_element_type=F32)
    scale = HEAD_DIM ** -0.5 * LOG2E
    o = 0
    for ref, sc in ((qa_ref, scale), (ka_ref, None), (qb_ref, scale), (kb_ref, None)):
        w = ref.shape[-1]
        piece = y[:, o:o + w]
        if sc is not None:
            piece = piece * sc
        ref[...] = piece.astype(BF16)
        o += w
    vt = lax.dot_general(wvt_ref[...], h, (((1,), (1,)), ((), ())),
                         preferred_element_type=F32).astype(BF16)
    va_rows = vat_ref.shape[1]
    vat_ref[0] = vt[:va_rows]
    vbt_ref[0, 0] = vt[va_rows:]


def _proj(x2, g, w_main, w_vt, batch, seq):
    t, d = x2.shape
    tm = PROJ_ROWS
    ns = seq // tm
    qa_w = SWA_Q_HEADS * HEAD_DIM
    ka_w = SWA_KV_HEADS * HEAD_DIM
    qb_w = DIFF_HEADS * 2 * HEAD_DIM
    vb_w = DIFF_HEADS * DIFF_V_DIM
    widths = (qa_w, ka_w, qb_w, qb_w)
    out_shape = [jax.ShapeDtypeStruct((t, w), BF16) for w in widths]
    out_shape.append(jax.ShapeDtypeStruct((batch, ka_w, seq), BF16))
    out_shape.append(jax.ShapeDtypeStruct((batch, ns, vb_w, tm), BF16))
    out_specs = [pl.BlockSpec((tm, w), lambda i: (i, 0)) for w in widths]
    out_specs.append(pl.BlockSpec((1, ka_w, tm), lambda i: (i // ns, 0, i % ns)))
    out_specs.append(pl.BlockSpec((1, 1, vb_w, tm), lambda i: (i // ns, i % ns, 0, 0)))
    return pl.pallas_call(
        _proj_kernel,
        grid=(t // tm,),
        in_specs=[pl.BlockSpec((tm, d), lambda i: (i, 0)),
                  _const_spec(g.shape), _const_spec(w_main.shape), _const_spec(w_vt.shape)],
        out_specs=out_specs,
        out_shape=out_shape,
        compiler_params=_params(1),
        name="proj_in",
    )(x2, g, w_main, w_vt)


def _swa_q_columns():
    cols = []
    for t in range(SWA_GROUP):
        for g in range(SWA_KV_HEADS):
            head = g * SWA_GROUP + t
            cols.extend(range(head * HEAD_DIM, (head + 1) * HEAD_DIM))
    return np.array(cols, dtype=np.int32)


def _swa_bias(slopes):
    qpos = np.arange(WINDOW)[None, :] + WINDOW
    kpos = np.arange(2 * WINDOW)[:, None]
    dist = (qpos - kpos).astype(np.float32)
    valid = (dist >= 0) & (dist < WINDOW)
    first_valid = valid & (kpos >= WINDOW)
    out = np.empty((2, SWA_KV_HEADS, 2 * WINDOW, SWA_GROUP * WINDOW), np.float32)
    for g in range(SWA_KV_HEADS):
        for t in range(SWA_GROUP):
            sl = np.float32(slopes[g * SWA_GROUP + t] * LOG2E)
            cols = slice(t * WINDOW, (t + 1) * WINDOW)
            out[0, g, :, cols] = np.where(valid, -sl * dist, NEG)
            out[1, g, :, cols] = np.where(first_valid, -sl * dist, NEG)
    return out


def _swa_kernel(q_ref, k_ref, kp_ref, vt_ref, vtp_ref, bias_ref, sink_ref, g_ref, o_ref):
    i = pl.program_id(1)
    nblk = q_ref.shape[1] // WINDOW
    lane = lax.broadcasted_iota(jnp.int32, (WINDOW, LANES), 1)
    ones_rows = jnp.ones((DEN_ROWS, 2 * WINDOW), BF16)

    def scores(n, g):
        rows = slice(n * WINDOW, (n + 1) * WINDOW)
        if n == 0:
            kk = jnp.concatenate([kp_ref[0], k_ref[0, rows, :]], axis=0)
        else:
            kk = k_ref[0, (n - 1) * WINDOW:(n + 1) * WINDOW, :]
        keep = (lane < HEAD_DIM) if g == 0 else (lane >= HEAD_DIM)
        q_tiles = [q_ref[0, rows, t * LANES:(t + 1) * LANES] for t in range(SWA_GROUP)]
        qm = jnp.concatenate([jnp.where(keep, qt, jnp.zeros_like(qt)) for qt in q_tiles],
                             axis=0)
        s = lax.dot_general(kk, qm, (((1,), (1,)), ((), ())),
                            preferred_element_type=F32)
        if n == 0:
            bias = jnp.where(i == 0, bias_ref[1, g], bias_ref[0, g])
        else:
            bias = bias_ref[0, g]
        return s + bias

    def weighted_values(n, g, s):
        if n == 0:
            vvt = jnp.concatenate([vtp_ref[0], vt_ref[0, :, 0:WINDOW]], axis=1)
        else:
            vvt = vt_ref[0, :, (n - 1) * WINDOW:(n + 1) * WINDOW]
        sink = sink_ref[g]
        m = jnp.maximum(jnp.max(s, axis=0, keepdims=True), sink)
        p = jnp.exp2(s - m).astype(BF16)
        vx = jnp.concatenate([vvt[g * HEAD_DIM:(g + 1) * HEAD_DIM], ones_rows], axis=0)
        o = jnp.dot(vx, p, preferred_element_type=F32)
        l = o[HEAD_DIM:HEAD_DIM + 1] + jnp.exp2(sink - m)
        ot = o[0:HEAD_DIM] * (1.0 / l)
        return [ot[:, t * WINDOW:(t + 1) * WINDOW] for t in range(SWA_GROUP)]

    tasks = [(n, g) for n in range(nblk) for g in range(SWA_KV_HEADS)]
    s_next = scores(*tasks[0])
    pieces = []
    for idx, (n, g) in enumerate(tasks):
        s_cur = s_next
        if idx + 1 < len(tasks):
            s_next = scores(*tasks[idx + 1])
        pieces.extend(weighted_values(n, g, s_cur))
        if g == SWA_KV_HEADS - 1:
            y = jnp.concatenate(pieces, axis=0).T
            o_ref[0, n * WINDOW:(n + 1) * WINDOW, :] = _rms(y, g_ref[...]).astype(BF16)
            pieces = []


def _swa(qa, ka, vat, bias, sink_rows, g):
    b, s, qw = qa.shape
    kw = ka.shape[-1]
    tq = SWA_ROWS
    r = tq // WINDOW
    cur = lambda bi, i: (bi, i, 0)
    prev = lambda bi, i: (bi, jnp.maximum(i * r - 1, 0), 0)
    cur_t = lambda bi, i: (bi, 0, i)
    prev_t = lambda bi, i: (bi, 0, jnp.maximum(i * r - 1, 0))
    return pl.pallas_call(
        _swa_kernel,
        grid=(b, s // tq),
        in_specs=[pl.BlockSpec((1, tq, qw), cur),
                  pl.BlockSpec((1, tq, kw), cur), pl.BlockSpec((1, WINDOW, kw), prev),
                  pl.BlockSpec((1, kw, tq), cur_t), pl.BlockSpec((1, kw, WINDOW), prev_t),
                  _const_spec(bias.shape), _const_spec(sink_rows.shape), _const_spec(g.shape)],
        out_specs=pl.BlockSpec((1, tq, qw), cur),
        out_shape=jax.ShapeDtypeStruct((b, s, qw), BF16),
        compiler_params=_params(2),
        name="swa_attn",
    )(qa, ka, ka, vat, vat, bias, sink_rows, g)


N_BIAS_COLS = 3
M_INIT = -1e28
DEN_LIMIT = 2.0 ** 60


def _diff_kernel(slopes_ref, q_ref, k_ref, vt_ref, lq1_ref, lk1_ref, lq2_ref, lk2_ref, g_ref,
                 o_ref, qx_ref, kf_ref, pa_ref, pb_ref, aa_ref, ab_ref, acc_ref, m_ref, l_ref,
                 *, lambda_init):
    h = pl.program_id(1)
    i = pl.program_id(2)
    tq = q_ref.shape[1]
    tk = kf_ref.shape[0]
    w = q_ref.shape[2]
    slope2 = slopes_ref[h] * LOG2E
    ndiag = tq // tk
    assert tq == ndiag * tk and ndiag % 2 == 0

    @pl.when(i == 0)
    def _():
        kloc = lax.broadcasted_iota(jnp.int32, (tk, w), 0).astype(F32)
        col = lax.broadcasted_iota(jnp.int32, (tk, w), 1)
        rest = slope2 * kloc
        feat = jnp.zeros((tk, w), F32)
        for n in range(N_BIAS_COLS):
            part = rest.astype(BF16).astype(F32)
            feat = jnp.where(col == n, part, feat)
            rest = rest - part
        kf_ref[...] = feat.astype(BF16)

    q = q_ref[0]
    lane = lax.broadcasted_iota(jnp.int32, q.shape, 1)
    zero = jnp.zeros_like(q)
    ones_cols = jnp.where(lane < N_BIAS_COLS, 1.0, 0.0).astype(BF16)
    qx_ref[0] = jnp.concatenate([jnp.where(lane < HEAD_DIM, q, zero), ones_cols], axis=1)
    qx_ref[1] = jnp.concatenate([jnp.where(lane >= HEAD_DIM, q, zero), ones_cols], axis=1)
    def softmax(j, p_ref, alpha_ref, lo, masked, track_max):
        nq = tq - lo
        kt = k_ref[0, pl.ds(pl.multiple_of(j * tk, tk), tk), :]
        kx = jnp.concatenate([kt, kf_ref[...]], axis=1)
        c = slope2 * (j * tk).astype(F32)
        for mp in range(2):
            s = lax.dot_general(kx, qx_ref[mp, lo:tq, :], (((1,), (1,)), ((), ())),
                                preferred_element_type=F32)
            if masked:
                krow = lax.broadcasted_iota(jnp.int32, (tk, nq), 0)
                qcol = lax.broadcasted_iota(jnp.int32, (tk, nq), 1)
                s = jnp.where(krow <= qcol, s, NEG)
            m = m_ref[mp, :, lo:tq]
            if track_max:
                m_old = m
                m = jnp.maximum(m_old, jnp.max(s, axis=0, keepdims=True) + c)
                alpha_ref[mp, :, lo:tq] = jnp.exp2(m_old - m)
                m_ref[mp, :, lo:tq] = m
            p = jnp.exp2(s - (m - c))
            p_ref[mp, :, lo:tq] = p.astype(BF16)
            psum = jnp.sum(p, axis=0, keepdims=True)
            if track_max:
                l_ref[mp, :, lo:tq] = alpha_ref[mp, :, lo:tq] * l_ref[mp, :, lo:tq] + psum
            else:
                l_ref[mp, :, lo:tq] += psum

    def weighted_values(j, p_ref, alpha_ref, lo, rescale):
        vt = vt_ref[0, j]
        for mp in range(2):
            pv = jnp.dot(vt, p_ref[mp, :, lo:tq], preferred_element_type=F32)
            if rescale:
                acc_ref[mp, :, lo:tq] = alpha_ref[mp, :, lo:tq] * acc_ref[mp, :, lo:tq] + pv
            else:
                acc_ref[mp, :, lo:tq] += pv

    def attend(track_max):
        acc_ref[...] = jnp.zeros_like(acc_ref)
        l_ref[...] = jnp.zeros_like(l_ref)
        bufs = ((pa_ref, aa_ref), (pb_ref, ab_ref))
        first = ndiag * i
        for step, d in enumerate(range(ndiag - 1, -1, -1)):
            softmax(first + d, *bufs[step % 2], d * tk, True, track_max)
            if step > 0:
                weighted_values(first + d + 1, *bufs[(step - 1) % 2], (d + 1) * tk, track_max)

        def pair(t, carry):
            j = first - 1 - 2 * t
            softmax(j, pa_ref, aa_ref, 0, False, track_max)
            weighted_values(j + 1, pb_ref, ab_ref, 0, track_max)
            softmax(j - 1, pb_ref, ab_ref, 0, False, track_max)
            weighted_values(j, pa_ref, aa_ref, 0, track_max)
            return carry

        lax.fori_loop(0, first // 2, pair, 0)
        weighted_values(0, pb_ref, ab_ref, 0, track_max)

    k_self = k_ref[0, pl.ds(pl.multiple_of(i * tq, tq), tq), :]
    qk = (q.astype(F32) * k_self.astype(F32)).astype(BF16)
    half = lax.broadcasted_iota(jnp.int32, (DEN_ROWS, w), 1) < HEAD_DIM
    qpos = (i * tq + lax.broadcasted_iota(jnp.int32, (1, tq), 1)).astype(F32)
    for mp in range(2):
        pick = jnp.where(half if mp == 0 else jnp.logical_not(half), 1.0, 0.0).astype(BF16)
        self_score = lax.dot_general(pick, qk, (((1,), (1,)), ((), ())),
                                     preferred_element_type=F32)[0:1]
        m_ref[mp] = self_score + slope2 * qpos
    attend(False)
    den_max = jnp.max(jnp.maximum(l_ref[0], l_ref[1]))

    @pl.when(jnp.logical_not(den_max <= DEN_LIMIT))
    def _():
        m_ref[...] = jnp.full_like(m_ref, M_INIT)
        attend(True)

    lam =(jnp.exp(jnp.sum(lq1_ref[...] * lk1_ref[...], axis=-1, keepdims=True))
           - jnp.exp(jnp.sum(lq2_ref[...] * lk2_ref[...], axis=-1, keepdims=True))
           + lambda_init)
    y = acc_ref[0] * (1.0 / l_ref[0]) - lam * (acc_ref[1] * (1.0 / l_ref[1]))
    yt = y.T
    o_ref[0] = (_rms(yt, g_ref[...]) * (1.0 - lambda_init)).astype(BF16)


def _diff(qb, kb, vbt, slopes, lq1, lk1, lq2, lk2, g, lambda_init):
    b, s, _ = qb.shape
    tq, tk = DIFF_TQ, DIFF_TK
    w = DIFF_V_DIM
    ntk = vbt.shape[1]
    grid_spec = pltpu.PrefetchScalarGridSpec(
        num_scalar_prefetch=1,
        grid=(b, DIFF_HEADS, s // tq),
        in_specs=[pl.BlockSpec((1, tq, w), lambda bi, h, i, sl: (bi, i, h)),
                  pl.BlockSpec((1, s, w), lambda bi, h, i, sl: (bi, 0, h)),
                  pl.BlockSpec((1, ntk, w, tk), lambda bi, h, i, sl: (bi, 0, h, 0)),
                  _const_spec(lq1.shape), _const_spec(lk1.shape),
                  _const_spec(lq2.shape), _const_spec(lk2.shape), _const_spec(g.shape)],
        out_specs=pl.BlockSpec((1, tq, w), lambda bi, h, i, sl: (bi, i, h)),
        scratch_shapes=[pltpu.VMEM((2, tq, 2 * w), BF16),
                        pltpu.VMEM((tk, w), BF16),
                        pltpu.VMEM((2, tk, tq), BF16),
                        pltpu.VMEM((2, tk, tq), BF16),
                        pltpu.VMEM((2, 1, tq), F32),
                        pltpu.VMEM((2, 1, tq), F32),
                        pltpu.VMEM((2, w, tq), F32),
                        pltpu.VMEM((2, 1, tq), F32),
                        pltpu.VMEM((2, 1, tq), F32)])
    return pl.pallas_call(
        functools.partial(_diff_kernel, lambda_init=lambda_init),
        grid_spec=grid_spec,
        out_shape=jax.ShapeDtypeStruct((b, s, DIFF_HEADS * w), BF16),
        compiler_params=_params(3),
        name="diff_attn",
    )(slopes, qb, kb, vbt, lq1, lk1, lq2, lk2, g)


def _outproj_kernel(ya_ref, yb_ref, x_ref, w_ref, g_ref, o_ref):
    ka = ya_ref.shape[-1]
    o = jnp.dot(ya_ref[...], w_ref[0:ka, :], preferred_element_type=F32)
    o = o + jnp.dot(yb_ref[...], w_ref[ka:, :], preferred_element_type=F32)
    o_ref[...] = x_ref[...] + _rms(o, g_ref[...])


def _outproj(ya, yb, x2, w_out, g):
    t, d = x2.shape
    tm = PROJ_ROWS
    return pl.pallas_call(
        _outproj_kernel,
        grid=(t // tm,),
        in_specs=[pl.BlockSpec((tm, ya.shape[-1]), lambda i: (i, 0)),
                  pl.BlockSpec((tm, yb.shape[-1]), lambda i: (i, 0)),
                  pl.BlockSpec((tm, d), lambda i: (i, 0)),
                  _const_spec(w_out.shape), _const_spec(g.shape)],
        out_specs=pl.BlockSpec((tm, d), lambda i: (i, 0)),
        out_shape=jax.ShapeDtypeStruct((t, d), F32),
        compiler_params=_params(1),
        name="proj_out",
    )(ya, yb, x2, w_out, g)


def _gelu_tanh(x):
    cdf = 0.5 * (1.0 + jnp.tanh(math.sqrt(2.0 / math.pi) * (x + 0.044715 * (x * x * x))))
    return x * cdf


def _ffn_kernel(x_ref, gpre_ref, wup_ref, cw_ref, wd_ref, gpost_ref,
                o_ref, h_ref, u_ref, tail_ref, acta_ref, actb_ref, acc_ref):
    i = pl.program_id(1)
    ts = x_ref.shape[1]
    d_ff = wd_ref.shape[0]
    fc = acta_ref.shape[-1]
    nlt = fc // LANES
    nchunk = d_ff // fc
    assert nchunk % 2 == 0

    @pl.when(i == 0)
    def _():
        tail_ref[...] = jnp.zeros_like(tail_ref)

    h_ref[...] = _rms(x_ref[0], gpre_ref[...]).astype(BF16)

    def cols(c, half):
        return pl.ds(pl.multiple_of(half * d_ff + c * fc, fc), fc)

    def up_act(c, act_ref):
        hh = h_ref[...]
        outs = []
        for half in range(2):
            u = jnp.dot(hh, wup_ref[:, cols(c, half)], preferred_element_type=F32)
            for lt in range(nlt):
                slab = half * nlt + lt
                piece = u[:, lt * LANES:(lt + 1) * LANES]
                u_ref[slab, 0:HALO] = tail_ref[c * (2 * nlt) + slab]
                u_ref[slab, HALO:] = piece
                tail_ref[c * (2 * nlt) + slab] = piece[ts - HALO:]
            cp = cw_ref[:, cols(c, half)]
            out = cp[CONV_WIDTH:CONV_WIDTH + 1]
            for tap in range(CONV_WIDTH):
                start = HALO - (CONV_WIDTH - 1) + tap
                shifted = jnp.concatenate(
                    [u_ref[pl.ds(half * nlt + lt, 1, stride=2), pl.ds(start, ts), :].reshape(ts, LANES)
                     for lt in range(nlt)], axis=1)
                out = out + cp[tap:tap + 1] * shifted
            outs.append(out)
        act_ref[...] = (_gelu_tanh(outs[0]) * outs[1]).astype(BF16)

    def down(c, act_ref):
        rows = pl.ds(pl.multiple_of(c * fc, fc), fc)
        acc_ref[...] += jnp.dot(act_ref[...], wd_ref[rows, :], preferred_element_type=F32)

    acc_ref[...] = jnp.zeros_like(acc_ref)
    up_act(0, acta_ref)

    def body(t, carry):
        c = 2 * t + 1
        up_act(c, actb_ref)
        down(c - 1, acta_ref)
        up_act(c + 1, acta_ref)
        down(c, actb_ref)
        return carry

    lax.fori_loop(0, (nchunk - 2) // 2, body, 0)
    up_act(nchunk - 1, actb_ref)
    down(nchunk - 2, acta_ref)
    down(nchunk - 1, actb_ref)

    o_ref[0] = x_ref[0] + _rms(acc_ref[...], gpost_ref[...])


def _ffn(x3, gpre, w_up, cw, w_down, gpost):
    b, s, d = x3.shape
    ts = FFN_ROWS
    fc = FFN_CHUNK
    nslab = 2 * fc // LANES
    nchunk = w_down.shape[0] // fc
    return pl.pallas_call(
        _ffn_kernel,
        grid=(b, s // ts),
        in_specs=[pl.BlockSpec((1, ts, d), lambda bi, i: (bi, i, 0)),
                  _const_spec(gpre.shape), _const_spec(w_up.shape), _const_spec(cw.shape),
                  _const_spec(w_down.shape), _const_spec(gpost.shape)],
        out_specs=pl.BlockSpec((1, ts, d), lambda bi, i: (bi, i, 0)),
        out_shape=jax.ShapeDtypeStruct((b, s, d), F32),
        scratch_shapes=[pltpu.VMEM((ts, d), BF16),
                        pltpu.VMEM((nslab, HALO + ts, LANES), F32),
                        pltpu.VMEM((nchunk * nslab, HALO, LANES), F32),
                        pltpu.VMEM((ts, fc), BF16),
                        pltpu.VMEM((ts, fc), BF16),
                        pltpu.VMEM((ts, d), F32)],
        compiler_params=_params(2),
        name="conv_ffn",
    )(x3, gpre, w_up, cw, w_down, gpost)


def kernel(x, attn_pre_g, w_in, swa_sinks, swa_out_g, diff_lq1, diff_lk1, diff_lq2, diff_lk2,
           diff_subln_g, w_out, attn_post_g, ffn_pre_g, w_up, conv_w, conv_b, w_down, ffn_post_g):
    batch, seq, d = x.shape
    depth = w_in.shape[0]
    slopes = _alibi_slopes(SWA_Q_HEADS + DIFF_HEADS)
    swa_bias = jnp.asarray(_swa_bias(slopes[:SWA_Q_HEADS]))
    diff_slopes = jnp.asarray(slopes[SWA_Q_HEADS:])
    qa_cols = _swa_q_columns()
    row = lambda v: v.reshape(1, -1).astype(F32)
    qa_w = SWA_Q_HEADS * HEAD_DIM
    kv_w = SWA_KV_HEADS * HEAD_DIM
    qb_w = DIFF_HEADS * 2 * HEAD_DIM
    va0 = qa_w + kv_w
    qb0 = va0 + kv_w
    vb0 = qb0 + 2 * qb_w

    for layer in range(depth):
        lambda_init = 0.8 - 0.6 * math.exp(-0.3 * layer)
        wl = w_in[layer]
        w_main = jnp.concatenate(
            [wl[:, qa_cols], wl[:, qa_w:va0], wl[:, qb0:vb0]], axis=1).astype(BF16)
        w_vt = jnp.concatenate([wl[:, va0:qb0], wl[:, vb0:]], axis=1).T.astype(BF16)
        qa, ka, qb, kb, vat, vbt = _proj(x.reshape(batch * seq, d), row(attn_pre_g[layer]),
                                         w_main, w_vt, batch, seq)
        sink_rows = jnp.repeat(swa_sinks[layer].astype(F32) * LOG2E, WINDOW).reshape(
            SWA_KV_HEADS, 1, SWA_GROUP * WINDOW)
        ya = _swa(qa.reshape(batch, seq, -1), ka.reshape(batch, seq, -1), vat,
                  swa_bias, sink_rows, row(swa_out_g[layer]))
        yb = _diff(qb.reshape(batch, seq, -1), kb.reshape(batch, seq, -1), vbt, diff_slopes,
                   row(diff_lq1[layer]), row(diff_lk1[layer]), row(diff_lq2[layer]),
                   row(diff_lk2[layer]), row(diff_subln_g[layer]), lambda_init)
        x1 = _outproj(ya.reshape(batch * seq, -1), yb.reshape(batch * seq, -1),
                      x.reshape(batch * seq, d), w_out[layer].astype(BF16),
                      row(attn_post_g[layer]))
        cw = jnp.concatenate([conv_w[layer], conv_b[layer][None, :]], axis=0).astype(F32)
        cw = jnp.pad(cw, ((0, SUBLANES - cw.shape[0]), (0, 0)))
        x = _ffn(x1.reshape(batch, seq, d), row(ffn_pre_g[layer]), w_up[layer].astype(BF16), cw,
                 w_down[layer].astype(BF16), row(ffn_post_g[layer]))
    return x
```

```python
import functools
import math

import numpy as np
import jax
import jax.numpy as jnp
from jax import lax
from jax.experimental import pallas as pl
from jax.experimental.pallas import tpu as pltpu

F32 = jnp.float32
BF16 = jnp.bfloat16

EPS = 1e-6
HEAD_DIM = 64
SWA_Q_HEADS = 8
SWA_KV_HEADS = 2
SWA_GROUP = SWA_Q_HEADS // SWA_KV_HEADS
WINDOW = 128
DIFF_HEADS = 4
DIFF_V_DIM = 2 * HEAD_DIM
CONV_WIDTH = 3
NEG = -1e30
LOG2E = math.log2(math.e)

LANES = 128
SUBLANES = 8
DEN_ROWS = 16
VMEM_LIMIT_BYTES = 56 * 1024 * 1024

PROJ_IN_ROWS = 1024
PROJ_ROWS = 512
SWA_ROWS = 512
DIFF_TK = 512
DIFF_TQ = 4 * DIFF_TK
FFN_ROWS = 512
FFN_CHUNK = 512
HALO = SUBLANES


def _alibi_slopes(n):
    def pow2(m):
        start = 2.0 ** (-8.0 / m)
        return [start ** (i + 1) for i in range(m)]
    if math.log2(n).is_integer():
        s = pow2(n)
    else:
        c = 2 ** int(math.floor(math.log2(n)))
        s = pow2(c) + pow2(2 * c)[0::2][: n - c]
    return np.array(sorted(s, reverse=True), dtype=np.float32)


def _rms(xf, g):
    return xf * lax.rsqrt(jnp.mean(xf * xf, axis=-1, keepdims=True) + EPS) * g


def _params(n_axes, flags=None):
    return pltpu.CompilerParams(
        dimension_semantics=("arbitrary",) * n_axes,
        vmem_limit_bytes=VMEM_LIMIT_BYTES,
        flags=flags)


def _const_spec(shape):
    nd = len(shape)
    return pl.BlockSpec(shape, lambda *_: (0,) * nd, pipeline_mode=pl.Buffered(1))


def _proj_kernel(x_ref, g_ref, w_ref, wvt_ref, qa_ref, ka_ref, qb_ref, kb_ref, vat_ref, vbt_ref):
    h = _rms(x_ref[...], g_ref[...]).astype(BF16)
    y = jnp.dot(h, w_ref[...], preferred_element_type=F32)
    scale = HEAD_DIM ** -0.5 * LOG2E
    o = 0
    for ref, sc in ((qa_ref, scale), (ka_ref, None), (qb_ref, scale), (kb_ref, None)):
        w = ref.shape[-1]
        piece = y[:, o:o + w]
        if sc is not None:
            piece = piece * sc
        ref[...] = piece.astype(BF16)
        o += w
    vt = lax.dot_general(wvt_ref[...], h, (((1,), (1,)), ((), ())),
                         preferred_element_type=F32).astype(BF16)
    va_rows = vat_ref.shape[1]
    vat_ref[0] = vt[:va_rows]
    tk = vbt_ref.shape[-1]
    for r in range(vbt_ref.shape[1]):
        vbt_ref[0, r] = vt[va_rows:, r * tk:(r + 1) * tk]


def _proj(x2, g, w_main, w_vt, batch, seq):
    t, d = x2.shape
    tm = PROJ_IN_ROWS
    tk = DIFF_TK
    ns = seq // tm
    qa_w = SWA_Q_HEADS * HEAD_DIM
    ka_w = SWA_KV_HEADS * HEAD_DIM
    qb_w = DIFF_HEADS * 2 * HEAD_DIM
    vb_w = DIFF_HEADS * DIFF_V_DIM
    widths = (qa_w, ka_w, qb_w, qb_w)
    out_shape = [jax.ShapeDtypeStruct((t, w), BF16) for w in widths]
    out_shape.append(jax.ShapeDtypeStruct((batch, ka_w, seq), BF16))
    out_shape.append(jax.ShapeDtypeStruct((batch, seq // tk, vb_w, tk), BF16))
    out_specs = [pl.BlockSpec((tm, w), lambda i: (i, 0)) for w in widths]
    out_specs.append(pl.BlockSpec((1, ka_w, tm), lambda i: (i // ns, 0, i % ns)))
    out_specs.append(pl.BlockSpec((1, tm // tk, vb_w, tk), lambda i: (i // ns, i % ns, 0, 0)))
    return pl.pallas_call(
        _proj_kernel,
        grid=(t // tm,),
        in_specs=[pl.BlockSpec((tm, d), lambda i: (i, 0)),
                  _const_spec(g.shape), _const_spec(w_main.shape), _const_spec(w_vt.shape)],
        out_specs=out_specs,
        out_shape=out_shape,
        compiler_params=_params(1),
        name="proj_in",
    )(x2, g, w_main, w_vt)


def _swa_q_columns():
    cols = []
    for t in range(SWA_GROUP):
        for g in range(SWA_KV_HEADS):
            head = g * SWA_GROUP + t
            cols.extend(range(head * HEAD_DIM, (head + 1) * HEAD_DIM))
    return np.array(cols, dtype=np.int32)


def _swa_bias(slopes):
    qpos = np.arange(WINDOW)[None, :] + WINDOW
    kpos = np.arange(2 * WINDOW)[:, None]
    dist = (qpos - kpos).astype(np.float32)
    valid = (dist >= 0) & (dist < WINDOW)
    first_valid = valid & (kpos >= WINDOW)
    out = np.empty((2, SWA_KV_HEADS, 2 * WINDOW, SWA_GROUP * WINDOW), np.float32)
    for g in range(SWA_KV_HEADS):
        for t in range(SWA_GROUP):
            sl = np.float32(slopes[g * SWA_GROUP + t] * LOG2E)
            cols = slice(t * WINDOW, (t + 1) * WINDOW)
            out[0, g, :, cols] = np.where(valid, -sl * dist, NEG)
            out[1, g, :, cols] = np.where(first_valid, -sl * dist, NEG)
    return out


def _swa_kernel(q_ref, k_ref, kp_ref, vt_ref, vtp_ref, bias_ref, sink_ref, g_ref, o_ref):
    i = pl.program_id(1)
    nblk = q_ref.shape[1] // WINDOW
    lane = lax.broadcasted_iota(jnp.int32, (WINDOW, LANES), 1)
    ones_rows = jnp.ones((DEN_ROWS, 2 * WINDOW), BF16)

    def scores(n, g):
        rows = slice(n * WINDOW, (n + 1) * WINDOW)
        if n == 0:
            kk = jnp.concatenate([kp_ref[0], k_ref[0, rows, :]], axis=0)
        else:
            kk = k_ref[0, (n - 1) * WINDOW:(n + 1) * WINDOW, :]
        keep = (lane < HEAD_DIM) if g == 0 else (lane >= HEAD_DIM)
        q_tiles = [q_ref[0, rows, t * LANES:(t + 1) * LANES] for t in range(SWA_GROUP)]
        qm = jnp.concatenate([jnp.where(keep, qt, jnp.zeros_like(qt)) for qt in q_tiles],
                             axis=0)
        s = lax.dot_general(kk, qm, (((1,), (1,)), ((), ())),
                            preferred_element_type=F32)
        if n == 0:
            bias = jnp.where(i == 0, bias_ref[1, g], bias_ref[0, g])
        else:
            bias = bias_ref[0, g]
        return s + bias

    def weighted_values(n, g, s):
        if n == 0:
            vvt = jnp.concatenate([vtp_ref[0], vt_ref[0, :, 0:WINDOW]], axis=1)
        else:
            vvt = vt_ref[0, :, (n - 1) * WINDOW:(n + 1) * WINDOW]
        sink = sink_ref[g]
        m = jnp.maximum(jnp.max(s, axis=0, keepdims=True), sink)
        p = jnp.exp2(s - m).astype(BF16)
        vx = jnp.concatenate([vvt[g * HEAD_DIM:(g + 1) * HEAD_DIM], ones_rows], axis=0)
        o = jnp.dot(vx, p, preferred_element_type=F32)
        l = o[HEAD_DIM:HEAD_DIM + 1] + jnp.exp2(sink - m)
        ot = o[0:HEAD_DIM] * (1.0 / l)
        return [ot[:, t * WINDOW:(t + 1) * WINDOW] for t in range(SWA_GROUP)]

    tasks = [(n, g) for n in range(nblk) for g in range(SWA_KV_HEADS)]
    s_next = scores(*tasks[0])
    pieces = []
    for idx, (n, g) in enumerate(tasks):
        s_cur = s_next
        if idx + 1 < len(tasks):
            s_next = scores(*tasks[idx + 1])
        pieces.extend(weighted_values(n, g, s_cur))
        if g == SWA_KV_HEADS - 1:
            y = jnp.concatenate(pieces, axis=0).T
            o_ref[0, n * WINDOW:(n + 1) * WINDOW, :] = _rms(y, g_ref[...]).astype(BF16)
            pieces = []


def _swa(qa, ka, vat, bias, sink_rows, g):
    b, s, qw = qa.shape
    kw = ka.shape[-1]
    tq = SWA_ROWS
    r = tq // WINDOW
    cur = lambda bi, i: (bi, i, 0)
    prev = lambda bi, i: (bi, jnp.maximum(i * r - 1, 0), 0)
    cur_t = lambda bi, i: (bi, 0, i)
    prev_t = lambda bi, i: (bi, 0, jnp.maximum(i * r - 1, 0))
    return pl.pallas_call(
        _swa_kernel,
        grid=(b, s // tq),
        in_specs=[pl.BlockSpec((1, tq, qw), cur),
                  pl.BlockSpec((1, tq, kw), cur), pl.BlockSpec((1, WINDOW, kw), prev),
                  pl.BlockSpec((1, kw, tq), cur_t), pl.BlockSpec((1, kw, WINDOW), prev_t),
                  _const_spec(bias.shape), _const_spec(sink_rows.shape), _const_spec(g.shape)],
        out_specs=pl.BlockSpec((1, tq, qw), cur),
        out_shape=jax.ShapeDtypeStruct((b, s, qw), BF16),
        compiler_params=_params(2),
        name="swa_attn",
    )(qa, ka, ka, vat, vat, bias, sink_rows, g)


N_BIAS_COLS = 3
M_INIT = -1e28
DEN_LIMIT = 2.0 ** 60


def _diff_kernel(slopes_ref, q_ref, k_ref, vt_ref, lq1_ref, lk1_ref, lq2_ref, lk2_ref, g_ref,
                 o_ref, qx_ref, kf_ref, pa_ref, pb_ref, aa_ref, ab_ref, acc_ref, m_ref, l_ref,
                 *, lambda_init):
    h = pl.program_id(1)
    i = pl.program_id(2)
    tq = q_ref.shape[1]
    tk = kf_ref.shape[0]
    w = q_ref.shape[2]
    slope2 = slopes_ref[h] * LOG2E
    ndiag = tq // tk
    assert tq == ndiag * tk and ndiag % 2 == 0

    @pl.when(i == 0)
    def _():
        kloc = lax.broadcasted_iota(jnp.int32, (tk, w), 0).astype(F32)
        col = lax.broadcasted_iota(jnp.int32, (tk, w), 1)
        rest = slope2 * kloc
        feat = jnp.zeros((tk, w), F32)
        for n in range(N_BIAS_COLS):
            part = rest.astype(BF16).astype(F32)
            feat = jnp.where(col == n, part, feat)
            rest = rest - part
        kf_ref[...] = feat.astype(BF16)

    q = q_ref[0]
    lane = lax.broadcasted_iota(jnp.int32, q.shape, 1)
    zero = jnp.zeros_like(q)
    ones_cols = jnp.where(lane < N_BIAS_COLS, 1.0, 0.0).astype(BF16)
    qx_ref[0] = jnp.concatenate([jnp.where(lane < HEAD_DIM, q, zero), ones_cols], axis=1)
    qx_ref[1] = jnp.concatenate([jnp.where(lane >= HEAD_DIM, q, zero), ones_cols], axis=1)
    def softmax(j, p_ref, alpha_ref, lo, masked, track_max):
        nq = tq - lo
        kt = k_ref[0, pl.ds(pl.multiple_of(j * tk, tk), tk), :]
        kx = jnp.concatenate([kt, kf_ref[...]], axis=1)
        c = slope2 * (j * tk).astype(F32)
        for mp in range(2):
            s = lax.dot_general(kx, qx_ref[mp, lo:tq, :], (((1,), (1,)), ((), ())),
                                preferred_element_type=F32)
            if masked:
                krow = lax.broadcasted_iota(jnp.int32, (tk, nq), 0)
                qcol = lax.broadcasted_iota(jnp.int32, (tk, nq), 1)
                s = jnp.where(krow <= qcol, s, NEG)
            m = m_ref[mp, :, lo:tq]
            if track_max:
                m_old = m
                m = jnp.maximum(m_old, jnp.max(s, axis=0, keepdims=True) + c)
                alpha_ref[mp, :, lo:tq] = jnp.exp2(m_old - m)
                m_ref[mp, :, lo:tq] = m
            p = jnp.exp2(s - (m - c))
            p_ref[mp, :, lo:tq] = p.astype(BF16)
            psum = jnp.sum(p, axis=0, keepdims=True)
            if track_max:
                l_ref[mp, :, lo:tq] = alpha_ref[mp, :, lo:tq] * l_ref[mp, :, lo:tq] + psum
            else:
                l_ref[mp, :, lo:tq] += psum

    def weighted_values(j, p_ref, alpha_ref, lo, rescale):
        vt = vt_ref[0, j]
        for mp in range(2):
            pv = jnp.dot(vt, p_ref[mp, :, lo:tq], preferred_element_type=F32)
            if rescale:
                acc_ref[mp, :, lo:tq] = alpha_ref[mp, :, lo:tq] * acc_ref[mp, :, lo:tq] + pv
            else:
                acc_ref[mp, :, lo:tq] += pv

    def attend(track_max):
        acc_ref[...] = jnp.zeros_like(acc_ref)
        l_ref[...] = jnp.zeros_like(l_ref)
        bufs = ((pa_ref, aa_ref), (pb_ref, ab_ref))
        first = ndiag * i
        for step, d in enumerate(range(ndiag - 1, -1, -1)):
            softmax(first + d, *bufs[step % 2], d * tk, True, track_max)
            if step > 0:
                weighted_values(first + d + 1, *bufs[(step - 1) % 2], (d + 1) * tk, track_max)

        def pair(t, carry):
            j = first - 1 - 2 * t
            softmax(j, pa_ref, aa_ref, 0, False, track_max)
            weighted_values(j + 1, pb_ref, ab_ref, 0, track_max)
            softmax(j - 1, pb_ref, ab_ref, 0, False, track_max)
            weighted_values(j, pa_ref, aa_ref, 0, track_max)
            return carry

        lax.fori_loop(0, first // 2, pair, 0)

    k_self = k_ref[0, pl.ds(pl.multiple_of(i * tq, tq), tq), :]
    qk = (q.astype(F32) * k_self.astype(F32)).astype(BF16)
    half = lax.broadcasted_iota(jnp.int32, (DEN_ROWS, w), 1) < HEAD_DIM
    qpos = (i * tq + lax.broadcasted_iota(jnp.int32, (1, tq), 1)).astype(F32)
    for mp in range(2):
        pick = jnp.where(half if mp == 0 else jnp.logical_not(half), 1.0, 0.0).astype(BF16)
        self_score = lax.dot_general(pick, qk, (((1,), (1,)), ((), ())),
                                     preferred_element_type=F32)[0:1]
        m_ref[mp] = self_score + slope2 * qpos
    ab_ref[...] = jnp.ones_like(ab_ref)
    attend(False)
    den_max = jnp.max(jnp.maximum(l_ref[0], l_ref[1]))

    @pl.when(jnp.logical_not(den_max <= DEN_LIMIT))
    def _():
        m_ref[...] = jnp.full_like(m_ref, M_INIT)
        attend(True)

    weighted_values(0, pb_ref, ab_ref, 0, True)

    lam =(jnp.exp(jnp.sum(lq1_ref[...] * lk1_ref[...], axis=-1, keepdims=True))
           - jnp.exp(jnp.sum(lq2_ref[...] * lk2_ref[...], axis=-1, keepdims=True))
           + lambda_init)
    y = acc_ref[0] * (1.0 / l_ref[0]) - acc_ref[1] * (lam / l_ref[1])
    ynorm = y * lax.rsqrt(jnp.mean(y * y, axis=0, keepdims=True) + EPS) * g_ref[...]
    o_ref[0] = (ynorm * (1.0 - lambda_init)).astype(BF16)


def _diff(qb, kb, vbt, slopes, lq1, lk1, lq2, lk2, g, lambda_init):
    b, s, _ = qb.shape
    tq, tk = DIFF_TQ, DIFF_TK
    w = DIFF_V_DIM
    ntk = vbt.shape[1]
    grid_spec = pltpu.PrefetchScalarGridSpec(
        num_scalar_prefetch=1,
        grid=(b, DIFF_HEADS, s // tq),
        in_specs=[pl.BlockSpec((1, tq, w), lambda bi, h, i, sl: (bi, i, h)),
                  pl.BlockSpec((1, s, w), lambda bi, h, i, sl: (bi, 0, h)),
                  pl.BlockSpec((1, ntk, w, tk), lambda bi, h, i, sl: (bi, 0, h, 0)),
                  _const_spec(lq1.shape), _const_spec(lk1.shape),
                  _const_spec(lq2.shape), _const_spec(lk2.shape), _const_spec(g.shape)],
        out_specs=pl.BlockSpec((1, w, tq), lambda bi, h, i, sl: (bi, h, i)),
        scratch_shapes=[pltpu.VMEM((2, tq, 2 * w), BF16),
                        pltpu.VMEM((tk, w), BF16),
                        pltpu.VMEM((2, tk, tq), BF16),
                        pltpu.VMEM((2, tk, tq), BF16),
                        pltpu.VMEM((2, 1, tq), F32),
                        pltpu.VMEM((2, 1, tq), F32),
                        pltpu.VMEM((2, w, tq), F32),
                        pltpu.VMEM((2, 1, tq), F32),
                        pltpu.VMEM((2, 1, tq), F32)])
    return pl.pallas_call(
        functools.partial(_diff_kernel, lambda_init=lambda_init),
        grid_spec=grid_spec,
        out_shape=jax.ShapeDtypeStruct((b, DIFF_HEADS * w, s), BF16),
        compiler_params=_params(3),
        name="diff_attn",
    )(slopes, qb, kb, vbt, lq1, lk1, lq2, lk2, g)


def _outproj_kernel(ya_ref, ybt_ref, x_ref, w_ref, g_ref, o_ref):
    ka = ya_ref.shape[-1]
    o = jnp.dot(ya_ref[...], w_ref[0:ka, :], preferred_element_type=F32)
    o = o + lax.dot_general(ybt_ref[0], w_ref[ka:, :], (((0,), (0,)), ((), ())),
                            preferred_element_type=F32)
    o_ref[...] = x_ref[...] + _rms(o, g_ref[...])


def _outproj(ya, ybt, x2, w_out, g):
    t, d = x2.shape
    tm = PROJ_ROWS
    ns = ybt.shape[-1] // tm
    return pl.pallas_call(
        _outproj_kernel,
        grid=(t // tm,),
        in_specs=[pl.BlockSpec((tm, ya.shape[-1]), lambda i: (i, 0)),
                  pl.BlockSpec((1, ybt.shape[1], tm), lambda i: (i // ns, 0, i % ns)),
                  pl.BlockSpec((tm, d), lambda i: (i, 0)),
                  _const_spec(w_out.shape), _const_spec(g.shape)],
        out_specs=pl.BlockSpec((tm, d), lambda i: (i, 0)),
        out_shape=jax.ShapeDtypeStruct((t, d), F32),
        compiler_params=_params(1),
        name="proj_out",
    )(ya, ybt, x2, w_out, g)


def _gelu_tanh_gate(x, val):
    k0 = math.sqrt(2.0 / math.pi)
    inner = x * (k0 + (k0 * 0.044715) * (x * x))
    hv = (0.5 * x) * val
    return hv + hv * jnp.tanh(inner)


def _ffn_kernel(x_ref, gpre_ref, wup_ref, cw_ref, wd_ref, gpost_ref,
                o_ref, h_ref, u_ref, tail_ref, acta_ref, actb_ref, acc_ref):
    i = pl.program_id(1)
    ts = x_ref.shape[1]
    d_ff = wd_ref.shape[0]
    fc = acta_ref.shape[-1]
    nlt = fc // LANES
    nchunk = d_ff // fc
    assert nchunk % 2 == 0

    @pl.when(i == 0)
    def _():
        tail_ref[...] = jnp.zeros_like(tail_ref)

    h_ref[...] = _rms(x_ref[0], gpre_ref[...]).astype(BF16)

    def cols(c, half):
        return pl.ds(pl.multiple_of(half * d_ff + c * fc, fc), fc)

    def up_act(c, act_ref):
        hh = h_ref[...]
        outs = []
        for half in range(2):
            u = jnp.dot(hh, wup_ref[:, cols(c, half)], preferred_element_type=F32)
            for lt in range(nlt):
                slab = half * nlt + lt
                piece = u[:, lt * LANES:(lt + 1) * LANES]
                u_ref[slab, 0:HALO] = tail_ref[c * (2 * nlt) + slab]
                u_ref[slab, HALO:] = piece
                tail_ref[c * (2 * nlt) + slab] = piece[ts - HALO:]
            cp = cw_ref[:, cols(c, half)]
            out = cp[CONV_WIDTH:CONV_WIDTH + 1]
            for tap in range(CONV_WIDTH):
                start = HALO - (CONV_WIDTH - 1) + tap
                shifted = jnp.concatenate(
                    [u_ref[pl.ds(half * nlt + lt, 1, stride=2), pl.ds(start, ts), :].reshape(ts, LANES)
                     for lt in range(nlt)], axis=1)
                out = out + cp[tap:tap + 1] * shifted
            outs.append(out)
        act_ref[...] = _gelu_tanh_gate(outs[0], outs[1]).astype(BF16)

    def down(c, act_ref):
        rows = pl.ds(pl.multiple_of(c * fc, fc), fc)
        acc_ref[...] += jnp.dot(act_ref[...], wd_ref[rows, :], preferred_element_type=F32)

    acc_ref[...] = jnp.zeros_like(acc_ref)
    up_act(0, acta_ref)

    def body(t, carry):
        c = 2 * t + 1
        up_act(c, actb_ref)
        down(c - 1, acta_ref)
        up_act(c + 1, acta_ref)
        down(c, actb_ref)
        return carry

    lax.fori_loop(0, (nchunk - 2) // 2, body, 0)
    up_act(nchunk - 1, actb_ref)
    down(nchunk - 2, acta_ref)
    down(nchunk - 1, actb_ref)

    o_ref[0] = x_ref[0] + _rms(acc_ref[...], gpost_ref[...])


def _ffn(x3, gpre, w_up, cw, w_down, gpost):
    b, s, d = x3.shape
    ts = FFN_ROWS
    fc = FFN_CHUNK
    nslab = 2 * fc // LANES
    nchunk = w_down.shape[0] // fc
    return pl.pallas_call(
        _ffn_kernel,
        grid=(b, s // ts),
        in_specs=[pl.BlockSpec((1, ts, d), lambda bi, i: (bi, i, 0)),
                  _const_spec(gpre.shape), _const_spec(w_up.shape), _const_spec(cw.shape),
                  _const_spec(w_down.shape), _const_spec(gpost.shape)],
        out_specs=pl.BlockSpec((1, ts, d), lambda bi, i: (bi, i, 0)),
        out_shape=jax.ShapeDtypeStruct((b, s, d), F32),
        scratch_shapes=[pltpu.VMEM((ts, d), BF16),
                        pltpu.VMEM((nslab, HALO + ts, LANES), F32),
                        pltpu.VMEM((nchunk * nslab, HALO, LANES), F32),
                        pltpu.VMEM((ts, fc), BF16),
                        pltpu.VMEM((ts, fc), BF16),
                        pltpu.VMEM((ts, d), F32)],
        compiler_params=_params(2),
        name="conv_ffn",
    )(x3, gpre, w_up, cw, w_down, gpost)


def kernel(x, attn_pre_g, w_in, swa_sinks, swa_out_g, diff_lq1, diff_lk1, diff_lq2, diff_lk2,
           diff_subln_g, w_out, attn_post_g, ffn_pre_g, w_up, conv_w, conv_b, w_down, ffn_post_g):
    batch, seq, d = x.shape
    depth = w_in.shape[0]
    slopes = _alibi_slopes(SWA_Q_HEADS + DIFF_HEADS)
    swa_bias = jnp.asarray(_swa_bias(slopes[:SWA_Q_HEADS]))
    diff_slopes = jnp.asarray(slopes[SWA_Q_HEADS:])
    qa_cols = _swa_q_columns()
    row = lambda v: v.reshape(1, -1).astype(F32)
    qa_w = SWA_Q_HEADS * HEAD_DIM
    kv_w = SWA_KV_HEADS * HEAD_DIM
    qb_w = DIFF_HEADS * 2 * HEAD_DIM
    va0 = qa_w + kv_w
    qb0 = va0 + kv_w
    vb0 = qb0 + 2 * qb_w

    for layer in range(depth):
        lambda_init = 0.8 - 0.6 * math.exp(-0.3 * layer)
        wl = w_in[layer]
        w_main = jnp.concatenate(
            [wl[:, qa_cols], wl[:, qa_w:va0], wl[:, qb0:vb0]], axis=1).astype(BF16)
        w_vt = jnp.concatenate([wl[:, va0:qb0], wl[:, vb0:]], axis=1).T.astype(BF16)
        qa, ka, qb, kb, vat, vbt = _proj(x.reshape(batch * seq, d), row(attn_pre_g[layer]),
                                         w_main, w_vt, batch, seq)
        sink_rows = jnp.repeat(swa_sinks[layer].astype(F32) * LOG2E, WINDOW).reshape(
            SWA_KV_HEADS, 1, SWA_GROUP * WINDOW)
        ya = _swa(qa.reshape(batch, seq, -1), ka.reshape(batch, seq, -1), vat,
                  swa_bias, sink_rows, row(swa_out_g[layer]))
        yb = _diff(qb.reshape(batch, seq, -1), kb.reshape(batch, seq, -1), vbt, diff_slopes,
                   row(diff_lq1[layer]), row(diff_lk1[layer]), row(diff_lq2[layer]),
                   row(diff_lk2[layer]), diff_subln_g[layer].reshape(-1, 1).astype(F32), lambda_init)
        x1 = _outproj(ya.reshape(batch * seq, -1), yb,
                      x.reshape(batch * seq, d), w_out[layer].astype(BF16),
                      row(attn_post_g[layer]))
        cw = jnp.concatenate([conv_w[layer], conv_b[layer][None, :]], axis=0).astype(F32)
        cw = jnp.pad(cw, ((0, SUBLANES - cw.shape[0]), (0, 0)))
        x = _ffn(x1.reshape(batch, seq, d), row(ffn_pre_g[layer]), w_up[layer].astype(BF16), cw,
                 w_down[layer].astype(BF16), row(ffn_post_g[layer]))
    return x
```

```python
import functools
import math

import numpy as np
import jax
import jax.numpy as jnp
from jax import lax
from jax.experimental import pallas as pl
from jax.experimental.pallas import tpu as pltpu

F32 = jnp.float32
BF16 = jnp.bfloat16

EPS = 1e-6
HEAD_DIM = 64
SWA_Q_HEADS = 8
SWA_KV_HEADS = 2
SWA_GROUP = SWA_Q_HEADS // SWA_KV_HEADS
WINDOW = 128
DIFF_HEADS = 4
DIFF_V_DIM = 2 * HEAD_DIM
CONV_WIDTH = 3
NEG = -1e30
LOG2E = math.log2(math.e)

LANES = 128
SUBLANES = 8
DEN_ROWS = 16
VMEM_LIMIT_BYTES = 56 * 1024 * 1024

PROJ_IN_ROWS = 1024
SWA_ROWS = 512
DIFF_TK = 512
DIFF_TQ = 4 * DIFF_TK
FFN_ROWS = 512
FFN_CHUNK = 512
HALO = SUBLANES


def _alibi_slopes(n):
    def pow2(m):
        start = 2.0 ** (-8.0 / m)
        return [start ** (i + 1) for i in range(m)]
    if math.log2(n).is_integer():
        s = pow2(n)
    else:
        c = 2 ** int(math.floor(math.log2(n)))
        s = pow2(c) + pow2(2 * c)[0::2][: n - c]
    return np.array(sorted(s, reverse=True), dtype=np.float32)


def _rms(xf, g):
    return xf * lax.rsqrt(jnp.mean(xf * xf, axis=-1, keepdims=True) + EPS) * g


def _params(n_axes, flags=None):
    return pltpu.CompilerParams(
        dimension_semantics=("arbitrary",) * n_axes,
        vmem_limit_bytes=VMEM_LIMIT_BYTES,
        flags=flags)


def _const_spec(shape):
    nd = len(shape)
    return pl.BlockSpec(shape, lambda *_: (0,) * nd, pipeline_mode=pl.Buffered(1))


def _proj_kernel(x_ref, g_ref, w_ref, wvt_ref, qa_ref, ka_ref, qb_ref, kb_ref, vat_ref, vbt_ref):
    h = _rms(x_ref[...], g_ref[...]).astype(BF16)
    y = jnp.dot(h, w_ref[...], preferred_element_type=F32)
    scale = HEAD_DIM ** -0.5 * LOG2E
    o = 0
    for ref, sc in ((qa_ref, scale), (ka_ref, None), (qb_ref, scale), (kb_ref, None)):
        w = ref.shape[-1]
        piece = y[:, o:o + w]
        if sc is not None:
            piece = piece * sc
        ref[...] = piece.astype(BF16)
        o += w
    vt = lax.dot_general(wvt_ref[...], h, (((1,), (1,)), ((), ())),
                         preferred_element_type=F32).astype(BF16)
    va_rows = vat_ref.shape[1]
    vat_ref[0] = vt[:va_rows]
    tk = vbt_ref.shape[-1]
    for r in range(vbt_ref.shape[1]):
        vbt_ref[0, r] = vt[va_rows:, r * tk:(r + 1) * tk]


def _proj(x2, g, w_main, w_vt, batch, seq):
    t, d = x2.shape
    tm = PROJ_IN_ROWS
    tk = DIFF_TK
    ns = seq // tm
    qa_w = SWA_Q_HEADS * HEAD_DIM
    ka_w = SWA_KV_HEADS * HEAD_DIM
    qb_w = DIFF_HEADS * 2 * HEAD_DIM
    vb_w = DIFF_HEADS * DIFF_V_DIM
    widths = (qa_w, ka_w, qb_w, qb_w)
    out_shape = [jax.ShapeDtypeStruct((t, w), BF16) for w in widths]
    out_shape.append(jax.ShapeDtypeStruct((batch, ka_w, seq), BF16))
    out_shape.append(jax.ShapeDtypeStruct((batch, seq // tk, vb_w, tk), BF16))
    out_specs = [pl.BlockSpec((tm, w), lambda i: (i, 0)) for w in widths]
    out_specs.append(pl.BlockSpec((1, ka_w, tm), lambda i: (i // ns, 0, i % ns)))
    out_specs.append(pl.BlockSpec((1, tm // tk, vb_w, tk), lambda i: (i // ns, i % ns, 0, 0)))
    return pl.pallas_call(
        _proj_kernel,
        grid=(t // tm,),
        in_specs=[pl.BlockSpec((tm, d), lambda i: (i, 0)),
                  _const_spec(g.shape), _const_spec(w_main.shape), _const_spec(w_vt.shape)],
        out_specs=out_specs,
        out_shape=out_shape,
        compiler_params=_params(1),
        name="proj_in",
    )(x2, g, w_main, w_vt)


def _swa_q_columns():
    cols = []
    for t in range(SWA_GROUP):
        for g in range(SWA_KV_HEADS):
            head = g * SWA_GROUP + t
            cols.extend(range(head * HEAD_DIM, (head + 1) * HEAD_DIM))
    return np.array(cols, dtype=np.int32)


def _swa_bias(slopes):
    qpos = np.arange(WINDOW)[None, :] + WINDOW
    kpos = np.arange(2 * WINDOW)[:, None]
    dist = (qpos - kpos).astype(np.float32)
    valid = (dist >= 0) & (dist < WINDOW)
    first_valid = valid & (kpos >= WINDOW)
    out = np.empty((2, SWA_KV_HEADS, 2 * WINDOW, SWA_GROUP * WINDOW), np.float32)
    for g in range(SWA_KV_HEADS):
        for t in range(SWA_GROUP):
            sl = np.float32(slopes[g * SWA_GROUP + t] * LOG2E)
            cols = slice(t * WINDOW, (t + 1) * WINDOW)
            out[0, g, :, cols] = np.where(valid, -sl * dist, NEG)
            out[1, g, :, cols] = np.where(first_valid, -sl * dist, NEG)
    return out


def _swa_kernel(q_ref, k_ref, kp_ref, vt_ref, vtp_ref, bias_ref, sink_ref, g_ref, o_ref):
    i = pl.program_id(1)
    nblk = q_ref.shape[1] // WINDOW
    lane = lax.broadcasted_iota(jnp.int32, (WINDOW, LANES), 1)
    ones_rows = jnp.ones((DEN_ROWS, 2 * WINDOW), BF16)

    def scores(n, g):
        rows = slice(n * WINDOW, (n + 1) * WINDOW)
        if n == 0:
            kk = jnp.concatenate([kp_ref[0], k_ref[0, rows, :]], axis=0)
        else:
            kk = k_ref[0, (n - 1) * WINDOW:(n + 1) * WINDOW, :]
        keep = (lane < HEAD_DIM) if g == 0 else (lane >= HEAD_DIM)
        q_tiles = [q_ref[0, rows, t * LANES:(t + 1) * LANES] for t in range(SWA_GROUP)]
        qm = jnp.concatenate([jnp.where(keep, qt, jnp.zeros_like(qt)) for qt in q_tiles],
                             axis=0)
        s = lax.dot_general(kk, qm, (((1,), (1,)), ((), ())),
                            preferred_element_type=F32)
        if n == 0:
            bias = jnp.where(i == 0, bias_ref[1, g], bias_ref[0, g])
        else:
            bias = bias_ref[0, g]
        return s + bias

    def weighted_values(n, g, s):
        if n == 0:
            vvt = jnp.concatenate([vtp_ref[0], vt_ref[0, :, 0:WINDOW]], axis=1)
        else:
            vvt = vt_ref[0, :, (n - 1) * WINDOW:(n + 1) * WINDOW]
        sink = sink_ref[g]
        m = jnp.maximum(jnp.max(s, axis=0, keepdims=True), sink)
        p = jnp.exp2(s - m).astype(BF16)
        vx = jnp.concatenate([vvt[g * HEAD_DIM:(g + 1) * HEAD_DIM], ones_rows], axis=0)
        o = jnp.dot(vx, p, preferred_element_type=F32)
        l = o[HEAD_DIM:HEAD_DIM + 1] + jnp.exp2(sink - m)
        ot = o[0:HEAD_DIM] * (1.0 / l)
        return [ot[:, t * WINDOW:(t + 1) * WINDOW] for t in range(SWA_GROUP)]

    tasks = [(n, g) for n in range(nblk) for g in range(SWA_KV_HEADS)]
    s_next = scores(*tasks[0])
    pieces = []
    for idx, (n, g) in enumerate(tasks):
        s_cur = s_next
        if idx + 1 < len(tasks):
            s_next = scores(*tasks[idx + 1])
        pieces.extend(weighted_values(n, g, s_cur))
        if g == SWA_KV_HEADS - 1:
            y = jnp.concatenate(pieces, axis=0).T
            o_ref[0, n * WINDOW:(n + 1) * WINDOW, :] = _rms(y, g_ref[...]).astype(BF16)
            pieces = []


def _swa(qa, ka, vat, bias, sink_rows, g):
    b, s, qw = qa.shape
    kw = ka.shape[-1]
    tq = SWA_ROWS
    r = tq // WINDOW
    cur = lambda bi, i: (bi, i, 0)
    prev = lambda bi, i: (bi, jnp.maximum(i * r - 1, 0), 0)
    cur_t = lambda bi, i: (bi, 0, i)
    prev_t = lambda bi, i: (bi, 0, jnp.maximum(i * r - 1, 0))
    return pl.pallas_call(
        _swa_kernel,
        grid=(b, s // tq),
        in_specs=[pl.BlockSpec((1, tq, qw), cur),
                  pl.BlockSpec((1, tq, kw), cur), pl.BlockSpec((1, WINDOW, kw), prev),
                  pl.BlockSpec((1, kw, tq), cur_t), pl.BlockSpec((1, kw, WINDOW), prev_t),
                  _const_spec(bias.shape), _const_spec(sink_rows.shape), _const_spec(g.shape)],
        out_specs=pl.BlockSpec((1, tq, qw), cur),
        out_shape=jax.ShapeDtypeStruct((b, s, qw), BF16),
        compiler_params=_params(2),
        name="swa_attn",
    )(qa, ka, ka, vat, vat, bias, sink_rows, g)


N_BIAS_COLS = 3
M_INIT = -1e28
DEN_LIMIT = 2.0 ** 60


def _diff_kernel(slopes_ref, q_ref, k_ref, vt_ref, lq1_ref, lk1_ref, lq2_ref, lk2_ref, g_ref,
                 o_ref, qx_ref, kf_ref, pa_ref, pb_ref, aa_ref, ab_ref, acc_ref, m_ref, l_ref,
                 *, lambda_init):
    h = pl.program_id(1)
    i = pl.program_id(2)
    tq = q_ref.shape[1]
    tk = kf_ref.shape[0]
    w = q_ref.shape[2]
    slope2 = slopes_ref[h] * LOG2E
    ndiag = tq // tk
    assert tq == ndiag * tk and ndiag % 2 == 0

    @pl.when(i == 0)
    def _():
        kloc = lax.broadcasted_iota(jnp.int32, (tk, w), 0).astype(F32)
        col = lax.broadcasted_iota(jnp.int32, (tk, w), 1)
        rest = slope2 * kloc
        feat = jnp.zeros((tk, w), F32)
        for n in range(N_BIAS_COLS):
            part = rest.astype(BF16).astype(F32)
            feat = jnp.where(col == n, part, feat)
            rest = rest - part
        kf_ref[...] = feat.astype(BF16)

    q = q_ref[0]
    lane = lax.broadcasted_iota(jnp.int32, q.shape, 1)
    zero = jnp.zeros_like(q)
    ones_cols = jnp.where(lane < N_BIAS_COLS, 1.0, 0.0).astype(BF16)
    qx_ref[0] = jnp.concatenate([jnp.where(lane < HEAD_DIM, q, zero), ones_cols], axis=1)
    qx_ref[1] = jnp.concatenate([jnp.where(lane >= HEAD_DIM, q, zero), ones_cols], axis=1)
    def softmax(j, p_ref, alpha_ref, lo, masked, track_max):
        nq = tq - lo
        kt = k_ref[0, pl.ds(pl.multiple_of(j * tk, tk), tk), :]
        kx = jnp.concatenate([kt, kf_ref[...]], axis=1)
        c = slope2 * (j * tk).astype(F32)
        for mp in range(2):
            s = lax.dot_general(kx, qx_ref[mp, lo:tq, :], (((1,), (1,)), ((), ())),
                                preferred_element_type=F32)
            if masked:
                krow = lax.broadcasted_iota(jnp.int32, (tk, nq), 0)
                qcol = lax.broadcasted_iota(jnp.int32, (tk, nq), 1)
                s = jnp.where(krow <= qcol, s, NEG)
            m = m_ref[mp, :, lo:tq]
            if track_max:
                m_old = m
                m = jnp.maximum(m_old, jnp.max(s, axis=0, keepdims=True) + c)
                alpha_ref[mp, :, lo:tq] = jnp.exp2(m_old - m)
                m_ref[mp, :, lo:tq] = m
            p = jnp.exp2(s - (m - c))
            p_ref[mp, :, lo:tq] = p.astype(BF16)
            psum = jnp.sum(p, axis=0, keepdims=True)
            if track_max:
                l_ref[mp, :, lo:tq] = alpha_ref[mp, :, lo:tq] * l_ref[mp, :, lo:tq] + psum
            else:
                l_ref[mp, :, lo:tq] += psum

    def weighted_values(j, p_ref, alpha_ref, lo, rescale):
        vt = vt_ref[0, j]
        for mp in range(2):
            pv = jnp.dot(vt, p_ref[mp, :, lo:tq], preferred_element_type=F32)
            if rescale:
                acc_ref[mp, :, lo:tq] = alpha_ref[mp, :, lo:tq] * acc_ref[mp, :, lo:tq] + pv
            else:
                acc_ref[mp, :, lo:tq] += pv

    def attend(track_max):
        acc_ref[...] = jnp.zeros_like(acc_ref)
        l_ref[...] = jnp.zeros_like(l_ref)
        bufs = ((pa_ref, aa_ref), (pb_ref, ab_ref))
        first = ndiag * i
        for step, d in enumerate(range(ndiag - 1, -1, -1)):
            softmax(first + d, *bufs[step % 2], d * tk, True, track_max)
            if step > 0:
                weighted_values(first + d + 1, *bufs[(step - 1) % 2], (d + 1) * tk, track_max)

        def pair(t, carry):
            j = first - 1 - 2 * t
            softmax(j, pa_ref, aa_ref, 0, False, track_max)
            weighted_values(j + 1, pb_ref, ab_ref, 0, track_max)
            softmax(j - 1, pb_ref, ab_ref, 0, False, track_max)
            weighted_values(j, pa_ref, aa_ref, 0, track_max)
            return carry

        lax.fori_loop(0, first // 2, pair, 0)

    k_self = k_ref[0, pl.ds(pl.multiple_of(i * tq, tq), tq), :]
    qk = (q.astype(F32) * k_self.astype(F32)).astype(BF16)
    half = lax.broadcasted_iota(jnp.int32, (DEN_ROWS, w), 1) < HEAD_DIM
    qpos = (i * tq + lax.broadcasted_iota(jnp.int32, (1, tq), 1)).astype(F32)
    for mp in range(2):
        pick = jnp.where(half if mp == 0 else jnp.logical_not(half), 1.0, 0.0).astype(BF16)
        self_score = lax.dot_general(pick, qk, (((1,), (1,)), ((), ())),
                                     preferred_element_type=F32)[0:1]
        m_ref[mp] = self_score + slope2 * qpos
    ab_ref[...] = jnp.ones_like(ab_ref)
    attend(False)
    den_max = jnp.max(jnp.maximum(l_ref[0], l_ref[1]))

    @pl.when(jnp.logical_not(den_max <= DEN_LIMIT))
    def _():
        m_ref[...] = jnp.full_like(m_ref, M_INIT)
        attend(True)

    weighted_values(0, pb_ref, ab_ref, 0, True)

    lam =(jnp.exp(jnp.sum(lq1_ref[...] * lk1_ref[...], axis=-1, keepdims=True))
           - jnp.exp(jnp.sum(lq2_ref[...] * lk2_ref[...], axis=-1, keepdims=True))
           + lambda_init)
    y = acc_ref[0] * (1.0 / l_ref[0]) - acc_ref[1] * (lam / l_ref[1])
    ynorm = y * lax.rsqrt(jnp.mean(y * y, axis=0, keepdims=True) + EPS) * g_ref[...]
    o_ref[0] = (ynorm * (1.0 - lambda_init)).astype(BF16)


def _diff(qb, kb, vbt, slopes, lq1, lk1, lq2, lk2, g, lambda_init):
    b, s, _ = qb.shape
    tq, tk = DIFF_TQ, DIFF_TK
    w = DIFF_V_DIM
    ntk = vbt.shape[1]
    grid_spec = pltpu.PrefetchScalarGridSpec(
        num_scalar_prefetch=1,
        grid=(b, DIFF_HEADS, s // tq),
        in_specs=[pl.BlockSpec((1, tq, w), lambda bi, h, i, sl: (bi, i, h)),
                  pl.BlockSpec((1, s, w), lambda bi, h, i, sl: (bi, 0, h)),
                  pl.BlockSpec((1, ntk, w, tk), lambda bi, h, i, sl: (bi, 0, h, 0)),
                  _const_spec(lq1.shape), _const_spec(lk1.shape),
                  _const_spec(lq2.shape), _const_spec(lk2.shape), _const_spec(g.shape)],
        out_specs=pl.BlockSpec((1, w, tq), lambda bi, h, i, sl: (bi, h, i)),
        scratch_shapes=[pltpu.VMEM((2, tq, 2 * w), BF16),
                        pltpu.VMEM((tk, w), BF16),
                        pltpu.VMEM((2, tk, tq), BF16),
                        pltpu.VMEM((2, tk, tq), BF16),
                        pltpu.VMEM((2, 1, tq), F32),
                        pltpu.VMEM((2, 1, tq), F32),
                        pltpu.VMEM((2, w, tq), F32),
                        pltpu.VMEM((2, 1, tq), F32),
                        pltpu.VMEM((2, 1, tq), F32)])
    return pl.pallas_call(
        functools.partial(_diff_kernel, lambda_init=lambda_init),
        grid_spec=grid_spec,
        out_shape=jax.ShapeDtypeStruct((b, DIFF_HEADS * w, s), BF16),
        compiler_params=_params(3),
        name="diff_attn",
    )(slopes, qb, kb, vbt, lq1, lk1, lq2, lk2, g)


def _gelu_tanh_gate(x, val):
    k0 = math.sqrt(2.0 / math.pi)
    inner = x * (k0 + (k0 * 0.044715) * (x * x))
    hv = (0.5 * x) * val
    return hv + hv * jnp.tanh(inner)


def _ffn_kernel(ya_ref, ybt_ref, x_ref, wout_ref, gatt_ref, gpre_ref, wup_ref, cw_ref, wd_ref,
                gpost_ref, o_ref, x1_ref, h_ref, u_ref, tail_ref, acta_ref, actb_ref, acc_ref):
    i = pl.program_id(1)
    ts = x_ref.shape[1]
    d_ff = wd_ref.shape[0]
    fc = acta_ref.shape[-1]
    nlt = fc // LANES
    nchunk = d_ff // fc
    assert nchunk % 2 == 0

    @pl.when(i == 0)
    def _():
        tail_ref[...] = jnp.zeros_like(tail_ref)

    ka = ya_ref.shape[-1]
    mix = jnp.dot(ya_ref[0], wout_ref[0:ka, :], preferred_element_type=F32)
    mix = mix + lax.dot_general(ybt_ref[0], wout_ref[ka:, :], (((0,), (0,)), ((), ())),
                                preferred_element_type=F32)
    x1_ref[...] = x_ref[0] + _rms(mix, gatt_ref[...])
    h_ref[...] = _rms(x1_ref[...], gpre_ref[...]).astype(BF16)

    def cols(c, half):
        return pl.ds(pl.multiple_of(half * d_ff + c * fc, fc), fc)

    def up_act(c, act_ref):
        hh = h_ref[...]
        outs = []
        for half in range(2):
            u = jnp.dot(hh, wup_ref[:, cols(c, half)], preferred_element_type=F32)
            for lt in range(nlt):
                slab = half * nlt + lt
                piece = u[:, lt * LANES:(lt + 1) * LANES]
                u_ref[slab, 0:HALO] = tail_ref[c * (2 * nlt) + slab]
                u_ref[slab, HALO:] = piece
                tail_ref[c * (2 * nlt) + slab] = piece[ts - HALO:]
            cp = cw_ref[:, cols(c, half)]
            out = cp[CONV_WIDTH:CONV_WIDTH + 1]
            for tap in range(CONV_WIDTH):
                start = HALO - (CONV_WIDTH - 1) + tap
                shifted = jnp.concatenate(
                    [u_ref[pl.ds(half * nlt + lt, 1, stride=2), pl.ds(start, ts), :].reshape(ts, LANES)
                     for lt in range(nlt)], axis=1)
                out = out + cp[tap:tap + 1] * shifted
            outs.append(out)
        act_ref[...] = _gelu_tanh_gate(outs[0], outs[1]).astype(BF16)

    def down(c, act_ref):
        rows = pl.ds(pl.multiple_of(c * fc, fc), fc)
        acc_ref[...] += jnp.dot(act_ref[...], wd_ref[rows, :], preferred_element_type=F32)

    acc_ref[...] = jnp.zeros_like(acc_ref)
    up_act(0, acta_ref)

    def body(t, carry):
        c = 2 * t + 1
        up_act(c, actb_ref)
        down(c - 1, acta_ref)
        up_act(c + 1, acta_ref)
        down(c, actb_ref)
        return carry

    lax.fori_loop(0, (nchunk - 2) // 2, body, 0)
    up_act(nchunk - 1, actb_ref)
    down(nchunk - 2, acta_ref)
    down(nchunk - 1, actb_ref)

    o_ref[0] = x1_ref[...] + _rms(acc_ref[...], gpost_ref[...])


def _outproj_ffn(ya, ybt, x3, w_out, gatt, gpre, w_up, cw, w_down, gpost):
    b, s, d = x3.shape
    ts = FFN_ROWS
    fc = FFN_CHUNK
    nslab = 2 * fc // LANES
    nchunk = w_down.shape[0] // fc
    consts = (w_out, gatt, gpre, w_up, cw, w_down, gpost)
    return pl.pallas_call(
        _ffn_kernel,
        grid=(b, s // ts),
        in_specs=[pl.BlockSpec((1, ts, ya.shape[-1]), lambda bi, i: (bi, i, 0)),
                  pl.BlockSpec((1, ybt.shape[1], ts), lambda bi, i: (bi, 0, i)),
                  pl.BlockSpec((1, ts, d), lambda bi, i: (bi, i, 0))]
                 + [_const_spec(c.shape) for c in consts],
        out_specs=pl.BlockSpec((1, ts, d), lambda bi, i: (bi, i, 0)),
        out_shape=jax.ShapeDtypeStruct((b, s, d), F32),
        scratch_shapes=[pltpu.VMEM((ts, d), F32),
                        pltpu.VMEM((ts, d), BF16),
                        pltpu.VMEM((nslab, HALO + ts, LANES), F32),
                        pltpu.VMEM((nchunk * nslab, HALO, LANES), F32),
                        pltpu.VMEM((ts, fc), BF16),
                        pltpu.VMEM((ts, fc), BF16),
                        pltpu.VMEM((ts, d), F32)],
        compiler_params=_params(2),
        name="outproj_ffn",
    )(ya, ybt, x3, *consts)


def kernel(x, attn_pre_g, w_in, swa_sinks, swa_out_g, diff_lq1, diff_lk1, diff_lq2, diff_lk2,
           diff_subln_g, w_out, attn_post_g, ffn_pre_g, w_up, conv_w, conv_b, w_down, ffn_post_g):
    batch, seq, d = x.shape
    depth = w_in.shape[0]
    slopes = _alibi_slopes(SWA_Q_HEADS + DIFF_HEADS)
    swa_bias = jnp.asarray(_swa_bias(slopes[:SWA_Q_HEADS]))
    diff_slopes = jnp.asarray(slopes[SWA_Q_HEADS:])
    qa_cols = _swa_q_columns()
    row = lambda v: v.reshape(1, -1).astype(F32)
    qa_w = SWA_Q_HEADS * HEAD_DIM
    kv_w = SWA_KV_HEADS * HEAD_DIM
    qb_w = DIFF_HEADS * 2 * HEAD_DIM
    va0 = qa_w + kv_w
    qb0 = va0 + kv_w
    vb0 = qb0 + 2 * qb_w

    for layer in range(depth):
        lambda_init = 0.8 - 0.6 * math.exp(-0.3 * layer)
        wl = w_in[layer]
        w_main = jnp.concatenate(
            [wl[:, qa_cols], wl[:, qa_w:va0], wl[:, qb0:vb0]], axis=1).astype(BF16)
        w_vt = jnp.concatenate([wl[:, va0:qb0], wl[:, vb0:]], axis=1).T.astype(BF16)
        qa, ka, qb, kb, vat, vbt = _proj(x.reshape(batch * seq, d), row(attn_pre_g[layer]),
                                         w_main, w_vt, batch, seq)
        sink_rows = jnp.repeat(swa_sinks[layer].astype(F32) * LOG2E, WINDOW).reshape(
            SWA_KV_HEADS, 1, SWA_GROUP * WINDOW)
        ya = _swa(qa.reshape(batch, seq, -1), ka.reshape(batch, seq, -1), vat,
                  swa_bias, sink_rows, row(swa_out_g[layer]))
        yb = _diff(qb.reshape(batch, seq, -1), kb.reshape(batch, seq, -1), vbt, diff_slopes,
                   row(diff_lq1[layer]), row(diff_lk1[layer]), row(diff_lq2[layer]),
                   row(diff_lk2[layer]), diff_subln_g[layer].reshape(-1, 1).astype(F32), lambda_init)
        cw = jnp.concatenate([conv_w[layer], conv_b[layer][None, :]], axis=0).astype(F32)
        cw = jnp.pad(cw, ((0, SUBLANES - cw.shape[0]), (0, 0)))
        x = _outproj_ffn(ya, yb, x, w_out[layer].astype(BF16), row(attn_post_g[layer]),
                         row(ffn_pre_g[layer]), w_up[layer].astype(BF16), cw,
                         w_down[layer].astype(BF16), row(ffn_post_g[layer]))
    return x
```

```python
import functools
import math

import numpy as np
import jax
import jax.numpy as jnp
from jax import lax
from jax.experimental import pallas as pl
from jax.experimental.pallas import tpu as pltpu

F32 = jnp.float32
BF16 = jnp.bfloat16

EPS = 1e-6
HEAD_DIM = 64
SWA_Q_HEADS = 8
SWA_KV_HEADS = 2
SWA_GROUP = SWA_Q_HEADS // SWA_KV_HEADS
WINDOW = 128
DIFF_HEADS = 4
DIFF_V_DIM = 2 * HEAD_DIM
CONV_WIDTH = 3
NEG = -1e30
LOG2E = math.log2(math.e)

LANES = 128
SUBLANES = 8
BF16_ROWS = 16
DEN_ROWS = BF16_ROWS
VMEM_LIMIT_BYTES = 56 * 1024 * 1024

PROJ_IN_ROWS = 1024
SWA_ROWS = 512
SWA_LOOKAHEAD = 2
DIFF_TK = 512
DIFF_TQ = 4 * DIFF_TK
FFN_ROWS = 512
OUTPROJ_ROWS = 256
FFN_CHUNK = 512
HALO = SUBLANES


def _alibi_slopes(n):
    def pow2(m):
        start = 2.0 ** (-8.0 / m)
        return [start ** (i + 1) for i in range(m)]
    if math.log2(n).is_integer():
        s = pow2(n)
    else:
        c = 2 ** int(math.floor(math.log2(n)))
        s = pow2(c) + pow2(2 * c)[0::2][: n - c]
    return np.array(sorted(s, reverse=True), dtype=np.float32)


def _rms(xf, g):
    return xf * lax.rsqrt(jnp.mean(xf * xf, axis=-1, keepdims=True) + EPS) * g


def _params(n_axes):
    return pltpu.CompilerParams(
        dimension_semantics=("arbitrary",) * n_axes,
        vmem_limit_bytes=VMEM_LIMIT_BYTES)


def _const_spec(shape):
    nd = len(shape)
    return pl.BlockSpec(shape, lambda *_: (0,) * nd, pipeline_mode=pl.Buffered(1))


def _proj_kernel(x_ref, g_ref, w_ref, wvt_ref, qa_ref, ka_ref, qb_ref, kb_ref, vat_ref, vbt_ref):
    h = _rms(x_ref[...], g_ref[...]).astype(BF16)
    y = jnp.dot(h, w_ref[...], preferred_element_type=F32)
    scale = HEAD_DIM ** -0.5 * LOG2E
    o = 0
    for ref, sc in ((qa_ref, scale), (ka_ref, None), (qb_ref, scale), (kb_ref, None)):
        w = ref.shape[-1]
        piece = y[:, o:o + w]
        if sc is not None:
            piece = piece * sc
        ref[...] = piece.astype(BF16)
        o += w
    vt = lax.dot_general(wvt_ref[...], h, (((1,), (1,)), ((), ())),
                         preferred_element_type=F32).astype(BF16)
    va_rows = vat_ref.shape[1]
    vat_ref[0] = vt[:va_rows]
    tk = vbt_ref.shape[-1]
    for r in range(vbt_ref.shape[1]):
        vbt_ref[0, r] = vt[va_rows:, r * tk:(r + 1) * tk]


def _proj(x2, g, w_main, w_vt, batch, seq):
    t, d = x2.shape
    tm = PROJ_IN_ROWS
    tk = DIFF_TK
    ns = seq // tm
    qa_w = SWA_Q_HEADS * HEAD_DIM
    ka_w = SWA_KV_HEADS * HEAD_DIM
    qb_w = DIFF_HEADS * 2 * HEAD_DIM
    vb_w = DIFF_HEADS * DIFF_V_DIM
    widths = (qa_w, ka_w, qb_w, qb_w)
    out_shape = [jax.ShapeDtypeStruct((t, w), BF16) for w in widths]
    out_shape.append(jax.ShapeDtypeStruct((batch, ka_w, seq), BF16))
    out_shape.append(jax.ShapeDtypeStruct((batch, seq // tk, vb_w, tk), BF16))
    out_specs = [pl.BlockSpec((tm, w), lambda i: (i, 0)) for w in widths]
    out_specs.append(pl.BlockSpec((1, ka_w, tm), lambda i: (i // ns, 0, i % ns)))
    out_specs.append(pl.BlockSpec((1, tm // tk, vb_w, tk), lambda i: (i // ns, i % ns, 0, 0)))
    return pl.pallas_call(
        _proj_kernel,
        grid=(t // tm,),
        in_specs=[pl.BlockSpec((tm, d), lambda i: (i, 0)),
                  _const_spec(g.shape), _const_spec(w_main.shape), _const_spec(w_vt.shape)],
        out_specs=out_specs,
        out_shape=out_shape,
        compiler_params=_params(1),
        name="proj_in",
    )(x2, g, w_main, w_vt)


def _swa_q_columns():
    cols = []
    for t in range(SWA_GROUP):
        for g in range(SWA_KV_HEADS):
            head = g * SWA_GROUP + t
            cols.extend(range(head * HEAD_DIM, (head + 1) * HEAD_DIM))
    return np.array(cols, dtype=np.int32)


def _swa_bias(slopes):
    qpos = np.arange(WINDOW)[None, :] + WINDOW
    kpos = np.arange(2 * WINDOW)[:, None]
    dist = (qpos - kpos).astype(np.float32)
    valid = (dist >= 0) & (dist < WINDOW)
    first_valid = valid & (kpos >= WINDOW)
    out = np.empty((2, SWA_KV_HEADS, 2 * WINDOW, SWA_GROUP * WINDOW), np.float32)
    for g in range(SWA_KV_HEADS):
        for t in range(SWA_GROUP):
            sl = np.float32(slopes[g * SWA_GROUP + t] * LOG2E)
            cols = slice(t * WINDOW, (t + 1) * WINDOW)
            out[0, g, :, cols] = np.where(valid, -sl * dist, NEG)
            out[1, g, :, cols] = np.where(first_valid, -sl * dist, NEG)
    return out


def _swa_kernel(q_ref, k_ref, kp_ref, vt_ref, vtp_ref, bias_ref, sink_ref, g_ref, o_ref):
    i = pl.program_id(1)
    nblk = q_ref.shape[1] // WINDOW
    lane = lax.broadcasted_iota(jnp.int32, (WINDOW, LANES), 1)
    ones_rows = jnp.ones((DEN_ROWS, 2 * WINDOW), BF16)

    def scores(n, g):
        rows = slice(n * WINDOW, (n + 1) * WINDOW)
        if n == 0:
            kk = jnp.concatenate([kp_ref[0], k_ref[0, rows, :]], axis=0)
        else:
            kk = k_ref[0, (n - 1) * WINDOW:(n + 1) * WINDOW, :]
        keep = (lane < HEAD_DIM) if g == 0 else (lane >= HEAD_DIM)
        q_tiles = [q_ref[0, rows, t * LANES:(t + 1) * LANES] for t in range(SWA_GROUP)]
        qm = jnp.concatenate([jnp.where(keep, qt, jnp.zeros_like(qt)) for qt in q_tiles],
                             axis=0)
        s = lax.dot_general(kk, qm, (((1,), (1,)), ((), ())),
                            preferred_element_type=F32)
        if n == 0:
            bias = jnp.where(i == 0, bias_ref[1, g], bias_ref[0, g])
        else:
            bias = bias_ref[0, g]
        return s + bias

    def weighted_values(n, g, s):
        if n == 0:
            vvt = jnp.concatenate([vtp_ref[0], vt_ref[0, :, 0:WINDOW]], axis=1)
        else:
            vvt = vt_ref[0, :, (n - 1) * WINDOW:(n + 1) * WINDOW]
        sink = sink_ref[g]
        m = jnp.maximum(jnp.max(s, axis=0, keepdims=True), sink)
        p = jnp.exp2(s - m).astype(BF16)
        vx = jnp.concatenate([vvt[g * HEAD_DIM:(g + 1) * HEAD_DIM], ones_rows], axis=0)
        o = jnp.dot(vx, p, preferred_element_type=F32)
        l = o[HEAD_DIM:HEAD_DIM + 1] + jnp.exp2(sink - m)
        ot = o[0:HEAD_DIM] * (1.0 / l)
        return [ot[:, t * WINDOW:(t + 1) * WINDOW] for t in range(SWA_GROUP)]

    tasks = [(n, g) for n in range(nblk) for g in range(SWA_KV_HEADS)]
    pending = [scores(*t) for t in tasks[:SWA_LOOKAHEAD]]
    pieces = []
    for idx, (n, g) in enumerate(tasks):
        s_cur = pending.pop(0)
        if idx + SWA_LOOKAHEAD < len(tasks):
            pending.append(scores(*tasks[idx + SWA_LOOKAHEAD]))
        pieces.extend(weighted_values(n, g, s_cur))
        if g == SWA_KV_HEADS - 1:
            y = jnp.concatenate(pieces, axis=0).T
            o_ref[0, n * WINDOW:(n + 1) * WINDOW, :] = _rms(y, g_ref[...]).astype(BF16)
            pieces = []


def _swa(qa, ka, vat, bias, sink_rows, g):
    b, s, qw = qa.shape
    kw = ka.shape[-1]
    tq = SWA_ROWS
    r = tq // WINDOW
    cur = lambda bi, i: (bi, i, 0)
    prev = lambda bi, i: (bi, jnp.maximum(i * r - 1, 0), 0)
    cur_t = lambda bi, i: (bi, 0, i)
    prev_t = lambda bi, i: (bi, 0, jnp.maximum(i * r - 1, 0))
    return pl.pallas_call(
        _swa_kernel,
        grid=(b, s // tq),
        in_specs=[pl.BlockSpec((1, tq, qw), cur),
                  pl.BlockSpec((1, tq, kw), cur), pl.BlockSpec((1, WINDOW, kw), prev),
                  pl.BlockSpec((1, kw, tq), cur_t), pl.BlockSpec((1, kw, WINDOW), prev_t),
                  _const_spec(bias.shape), _const_spec(sink_rows.shape), _const_spec(g.shape)],
        out_specs=pl.BlockSpec((1, tq, qw), cur),
        out_shape=jax.ShapeDtypeStruct((b, s, qw), BF16),
        compiler_params=_params(2),
        name="swa_attn",
    )(qa, ka, ka, vat, vat, bias, sink_rows, g)


N_BIAS_COLS = 3
M_INIT = -1e28
DEN_LIMIT = 2.0 ** 60


def _diff_kernel(slopes_ref, q_ref, k_ref, vt_ref, lq1_ref, lk1_ref, lq2_ref, lk2_ref, g_ref,
                 o_ref, qx_ref, kf_ref, pa_ref, pb_ref, aa_ref, ab_ref, acc_ref, m_ref, l_ref,
                 *, lambda_init):
    h = pl.program_id(1)
    i = pl.program_id(2)
    tq = q_ref.shape[1]
    tk = kf_ref.shape[0]
    w = q_ref.shape[2]
    slope2 = slopes_ref[h] * LOG2E
    ndiag = tq // tk
    assert tq == ndiag * tk and ndiag % 2 == 0

    @pl.when(i == 0)
    def _():
        kloc = lax.broadcasted_iota(jnp.int32, (tk, w), 0).astype(F32)
        col = lax.broadcasted_iota(jnp.int32, (tk, w), 1)
        rest = slope2 * kloc
        feat = jnp.zeros((tk, w), F32)
        for n in range(N_BIAS_COLS):
            part = rest.astype(BF16).astype(F32)
            feat = jnp.where(col == n, part, feat)
            rest = rest - part
        kf_ref[...] = feat.astype(BF16)

    q = q_ref[0]
    lane = lax.broadcasted_iota(jnp.int32, q.shape, 1)
    zero = jnp.zeros_like(q)
    ones_cols = jnp.where(lane < N_BIAS_COLS, 1.0, 0.0).astype(BF16)
    qx_ref[0] = jnp.concatenate([jnp.where(lane < HEAD_DIM, q, zero), ones_cols], axis=1)
    qx_ref[1] = jnp.concatenate([jnp.where(lane >= HEAD_DIM, q, zero), ones_cols], axis=1)
    def softmax(j, p_ref, alpha_ref, lo, masked, track_max):
        nq = tq - lo
        kt = k_ref[0, pl.ds(pl.multiple_of(j * tk, tk), tk), :]
        kx = jnp.concatenate([kt, kf_ref[...]], axis=1)
        c = slope2 * (j * tk).astype(F32)
        for mp in range(2):
            s = lax.dot_general(kx, qx_ref[mp, lo:tq, :], (((1,), (1,)), ((), ())),
                                preferred_element_type=F32)
            if masked:
                krow = lax.broadcasted_iota(jnp.int32, (tk, nq), 0)
                qcol = lax.broadcasted_iota(jnp.int32, (tk, nq), 1)
                s = jnp.where(krow <= qcol, s, NEG)
            m = m_ref[mp, :, lo:tq]
            if track_max:
                m_old = m
                m = jnp.maximum(m_old, jnp.max(s, axis=0, keepdims=True) + c)
                alpha_ref[mp, :, lo:tq] = jnp.exp2(m_old - m)
                m_ref[mp, :, lo:tq] = m
            p = jnp.exp2(s - (m - c))
            p_ref[mp, :, lo:tq] = p.astype(BF16)
            psum = jnp.sum(p, axis=0, keepdims=True)
            if track_max:
                l_ref[mp, :, lo:tq] = alpha_ref[mp, :, lo:tq] * l_ref[mp, :, lo:tq] + psum
            else:
                l_ref[mp, :, lo:tq] += psum

    def weighted_values(j, p_ref, alpha_ref, lo, rescale):
        vt = vt_ref[0, j]
        for mp in range(2):
            pv = jnp.dot(vt, p_ref[mp, :, lo:tq], preferred_element_type=F32)
            if rescale:
                acc_ref[mp, :, lo:tq] = alpha_ref[mp, :, lo:tq] * acc_ref[mp, :, lo:tq] + pv
            else:
                acc_ref[mp, :, lo:tq] += pv

    def attend(track_max):
        acc_ref[...] = jnp.zeros_like(acc_ref)
        l_ref[...] = jnp.zeros_like(l_ref)
        bufs = ((pa_ref, aa_ref), (pb_ref, ab_ref))
        first = ndiag * i
        for step, d in enumerate(range(ndiag - 1, -1, -1)):
            softmax(first + d, *bufs[step % 2], d * tk, True, track_max)
            if step > 0:
                weighted_values(first + d + 1, *bufs[(step - 1) % 2], (d + 1) * tk, track_max)

        def pair(t, carry):
            j = first - 1 - 2 * t
            softmax(j, pa_ref, aa_ref, 0, False, track_max)
            weighted_values(j + 1, pb_ref, ab_ref, 0, track_max)
            softmax(j - 1, pb_ref, ab_ref, 0, False, track_max)
            weighted_values(j, pa_ref, aa_ref, 0, track_max)
            return carry

        lax.fori_loop(0, first // 2, pair, 0)

    k_self = k_ref[0, pl.ds(pl.multiple_of(i * tq, tq), tq), :]
    qk = (q.astype(F32) * k_self.astype(F32)).astype(BF16)
    half = lax.broadcasted_iota(jnp.int32, (BF16_ROWS, w), 1) < HEAD_DIM
    qpos = (i * tq + lax.broadcasted_iota(jnp.int32, (1, tq), 1)).astype(F32)
    for mp in range(2):
        pick = jnp.where(half if mp == 0 else jnp.logical_not(half), 1.0, 0.0).astype(BF16)
        self_score = lax.dot_general(pick, qk, (((1,), (1,)), ((), ())),
                                     preferred_element_type=F32)[0:1]
        m_ref[mp] = self_score + slope2 * qpos
    ab_ref[...] = jnp.ones_like(ab_ref)
    attend(False)
    den_max = jnp.max(jnp.maximum(l_ref[0], l_ref[1]))

    @pl.when(jnp.logical_not(den_max <= DEN_LIMIT))
    def _():
        m_ref[...] = jnp.full_like(m_ref, M_INIT)
        attend(True)

    weighted_values(0, pb_ref, ab_ref, 0, True)

    lam =(jnp.exp(jnp.sum(lq1_ref[...] * lk1_ref[...], axis=-1, keepdims=True))
           - jnp.exp(jnp.sum(lq2_ref[...] * lk2_ref[...], axis=-1, keepdims=True))
           + lambda_init)
    y = acc_ref[0] * (1.0 / l_ref[0]) - acc_ref[1] * (lam / l_ref[1])
    ynorm = y * lax.rsqrt(jnp.mean(y * y, axis=0, keepdims=True) + EPS) * g_ref[...]
    o_ref[0] = (ynorm * (1.0 - lambda_init)).astype(BF16)


def _diff(qb, kb, vbt, slopes, lq1, lk1, lq2, lk2, g, lambda_init):
    b, s, _ = qb.shape
    tq, tk = DIFF_TQ, DIFF_TK
    w = DIFF_V_DIM
    ntk = vbt.shape[1]
    grid_spec = pltpu.PrefetchScalarGridSpec(
        num_scalar_prefetch=1,
        grid=(b, DIFF_HEADS, s // tq),
        in_specs=[pl.BlockSpec((1, tq, w), lambda bi, h, i, sl: (bi, i, h)),
                  pl.BlockSpec((1, s, w), lambda bi, h, i, sl: (bi, 0, h)),
                  pl.BlockSpec((1, ntk, w, tk), lambda bi, h, i, sl: (bi, 0, h, 0)),
                  _const_spec(lq1.shape), _const_spec(lk1.shape),
                  _const_spec(lq2.shape), _const_spec(lk2.shape), _const_spec(g.shape)],
        out_specs=pl.BlockSpec((1, w, tq), lambda bi, h, i, sl: (bi, h, i)),
        scratch_shapes=[pltpu.VMEM((2, tq, 2 * w), BF16),
                        pltpu.VMEM((tk, w), BF16),
                        pltpu.VMEM((2, tk, tq), BF16),
                        pltpu.VMEM((2, tk, tq), BF16),
                        pltpu.VMEM((2, 1, tq), F32),
                        pltpu.VMEM((2, 1, tq), F32),
                        pltpu.VMEM((2, w, tq), F32),
                        pltpu.VMEM((2, 1, tq), F32),
                        pltpu.VMEM((2, 1, tq), F32)])
    return pl.pallas_call(
        functools.partial(_diff_kernel, lambda_init=lambda_init),
        grid_spec=grid_spec,
        out_shape=jax.ShapeDtypeStruct((b, DIFF_HEADS * w, s), BF16),
        compiler_params=_params(3),
        name="diff_attn",
    )(slopes, qb, kb, vbt, lq1, lk1, lq2, lk2, g)


def _gelu_tanh_gate(x, val):
    k0 = math.sqrt(2.0 / math.pi)
    inner = x * (k0 + (k0 * 0.044715) * (x * x))
    hv = (0.5 * x) * val
    return hv + hv * jnp.tanh(inner)


def _ffn_kernel(ya_ref, ybt_ref, x_ref, wout_ref, gatt_ref, gpre_ref, wup_ref, cw_ref, wd_ref,
                gpost_ref, o_ref, x1_ref, h_ref, u_ref, tail_ref, acta_ref, actb_ref, acc_ref):
    i = pl.program_id(1)
    ts = x_ref.shape[1]
    d_ff = wd_ref.shape[0]
    fc = acta_ref.shape[-1]
    nlt = fc // LANES
    nchunk = d_ff // fc
    assert nchunk % 2 == 0

    @pl.when(i == 0)
    def _():
        tail_ref[...] = jnp.zeros_like(tail_ref)

    ka = ya_ref.shape[-1]
    for r in range(0, ts, OUTPROJ_ROWS):
        rows = slice(r, r + OUTPROJ_ROWS)
        mix = jnp.dot(ya_ref[0, rows, :], wout_ref[0:ka, :], preferred_element_type=F32)
        mix = mix + lax.dot_general(ybt_ref[0, :, rows], wout_ref[ka:, :],
                                    (((0,), (0,)), ((), ())), preferred_element_type=F32)
        x1 = x_ref[0, rows, :] + _rms(mix, gatt_ref[...])
        x1_ref[rows, :] = x1
        h_ref[rows, :] = _rms(x1, gpre_ref[...]).astype(BF16)

    def cols(c, half):
        return pl.ds(pl.multiple_of(half * d_ff + c * fc, fc), fc)

    def up_act(c, act_ref):
        hh = h_ref[...]
        outs = []
        for half in range(2):
            u = jnp.dot(hh, wup_ref[:, cols(c, half)], preferred_element_type=F32)
            for lt in range(nlt):
                slab = half * nlt + lt
                piece = u[:, lt * LANES:(lt + 1) * LANES]
                u_ref[slab, 0:HALO] = tail_ref[c * (2 * nlt) + slab]
                u_ref[slab, HALO:] = piece
                tail_ref[c * (2 * nlt) + slab] = piece[ts - HALO:]
            cp = cw_ref[:, cols(c, half)]
            out = cp[CONV_WIDTH:CONV_WIDTH + 1]
            for tap in range(CONV_WIDTH):
                start = HALO - (CONV_WIDTH - 1) + tap
                shifted = jnp.concatenate(
                    [u_ref[pl.ds(half * nlt + lt, 1, stride=2), pl.ds(start, ts), :].reshape(ts, LANES)
                     for lt in range(nlt)], axis=1)
                out = out + cp[tap:tap + 1] * shifted
            outs.append(out)
        act_ref[...] = _gelu_tanh_gate(outs[0], outs[1]).astype(BF16)

    def down(c, act_ref):
        rows = pl.ds(pl.multiple_of(c * fc, fc), fc)
        acc_ref[...] += jnp.dot(act_ref[...], wd_ref[rows, :], preferred_element_type=F32)

    acc_ref[...] = jnp.zeros_like(acc_ref)
    up_act(0, acta_ref)

    def body(t, carry):
        c = 2 * t + 1
        up_act(c, actb_ref)
        down(c - 1, acta_ref)
        up_act(c + 1, acta_ref)
        down(c, actb_ref)
        return carry

    lax.fori_loop(0, (nchunk - 2) // 2, body, 0)
    up_act(nchunk - 1, actb_ref)
    down(nchunk - 2, acta_ref)
    down(nchunk - 1, actb_ref)

    o_ref[0] = x1_ref[...] + _rms(acc_ref[...], gpost_ref[...])


def _outproj_ffn(ya, ybt, x3, w_out, gatt, gpre, w_up, cw, w_down, gpost):
    b, s, d = x3.shape
    ts = FFN_ROWS
    fc = FFN_CHUNK
    nslab = 2 * fc // LANES
    nchunk = w_down.shape[0] // fc
    consts = (w_out, gatt, gpre, w_up, cw, w_down, gpost)
    return pl.pallas_call(
        _ffn_kernel,
        grid=(b, s // ts),
        in_specs=[pl.BlockSpec((1, ts, ya.shape[-1]), lambda bi, i: (bi, i, 0)),
                  pl.BlockSpec((1, ybt.shape[1], ts), lambda bi, i: (bi, 0, i)),
                  pl.BlockSpec((1, ts, d), lambda bi, i: (bi, i, 0))]
                 + [_const_spec(c.shape) for c in consts],
        out_specs=pl.BlockSpec((1, ts, d), lambda bi, i: (bi, i, 0)),
        out_shape=jax.ShapeDtypeStruct((b, s, d), F32),
        scratch_shapes=[pltpu.VMEM((ts, d), F32),
                        pltpu.VMEM((ts, d), BF16),
                        pltpu.VMEM((nslab, HALO + ts, LANES), F32),
                        pltpu.VMEM((nchunk * nslab, HALO, LANES), F32),
                        pltpu.VMEM((ts, fc), BF16),
                        pltpu.VMEM((ts, fc), BF16),
                        pltpu.VMEM((ts, d), F32)],
        compiler_params=_params(2),
        name="outproj_ffn",
    )(ya, ybt, x3, *consts)


def kernel(x, attn_pre_g, w_in, swa_sinks, swa_out_g, diff_lq1, diff_lk1, diff_lq2, diff_lk2,
           diff_subln_g, w_out, attn_post_g, ffn_pre_g, w_up, conv_w, conv_b, w_down, ffn_post_g):
    batch, seq, d = x.shape
    depth = w_in.shape[0]
    slopes = _alibi_slopes(SWA_Q_HEADS + DIFF_HEADS)
    swa_bias = jnp.asarray(_swa_bias(slopes[:SWA_Q_HEADS]))
    diff_slopes = jnp.asarray(slopes[SWA_Q_HEADS:])
    qa_cols = _swa_q_columns()
    row = lambda v: v.reshape(1, -1).astype(F32)
    qa_w = SWA_Q_HEADS * HEAD_DIM
    kv_w = SWA_KV_HEADS * HEAD_DIM
    qb_w = DIFF_HEADS * 2 * HEAD_DIM
    va0 = qa_w + kv_w
    qb0 = va0 + kv_w
    vb0 = qb0 + 2 * qb_w

    for layer in range(depth):
        lambda_init = 0.8 - 0.6 * math.exp(-0.3 * layer)
        wl = w_in[layer]
        w_main = jnp.concatenate(
            [wl[:, qa_cols], wl[:, qa_w:va0], wl[:, qb0:vb0]], axis=1).astype(BF16)
        w_vt = jnp.concatenate([wl[:, va0:qb0], wl[:, vb0:]], axis=1).T.astype(BF16)
        qa, ka, qb, kb, vat, vbt = _proj(x.reshape(batch * seq, d), row(attn_pre_g[layer]),
                                         w_main, w_vt, batch, seq)
        sink_rows = jnp.repeat(swa_sinks[layer].astype(F32) * LOG2E, WINDOW).reshape(
            SWA_KV_HEADS, 1, SWA_GROUP * WINDOW)
        ya = _swa(qa.reshape(batch, seq, -1), ka.reshape(batch, seq, -1), vat,
                  swa_bias, sink_rows, row(swa_out_g[layer]))
        yb = _diff(qb.reshape(batch, seq, -1), kb.reshape(batch, seq, -1), vbt, diff_slopes,
                   row(diff_lq1[layer]), row(diff_lk1[layer]), row(diff_lq2[layer]),
                   row(diff_lk2[layer]), diff_subln_g[layer].reshape(-1, 1).astype(F32), lambda_init)
        cw = jnp.concatenate([conv_w[layer], conv_b[layer][None, :]], axis=0).astype(F32)
        cw = jnp.pad(cw, ((0, SUBLANES - cw.shape[0]), (0, 0)))
        x = _outproj_ffn(ya, yb, x, w_out[layer].astype(BF16), row(attn_post_g[layer]),
                         row(ffn_pre_g[layer]), w_up[layer].astype(BF16), cw,
                         w_down[layer].astype(BF16), row(ffn_post_g[layer]))
    return x
```

```python
import functools
import math

import numpy as np
import jax
import jax.numpy as jnp
from jax import lax
from jax.experimental import pallas as pl
from jax.experimental.pallas import tpu as pltpu

F32 = jnp.float32
BF16 = jnp.bfloat16

EPS = 1e-6
HEAD_DIM = 64
SWA_Q_HEADS = 8
SWA_KV_HEADS = 2
SWA_GROUP = SWA_Q_HEADS // SWA_KV_HEADS
WINDOW = 128
DIFF_HEADS = 4
DIFF_V_DIM = 2 * HEAD_DIM
CONV_WIDTH = 3
NEG = -1e30
LOG2E = math.log2(math.e)

LANES = 128
SUBLANES = 8
BF16_ROWS = 16
DEN_ROWS = BF16_ROWS
VMEM_LIMIT_BYTES = 56 * 1024 * 1024

PROJ_IN_ROWS = 1024
SWA_ROWS = 512
SWA_LOOKAHEAD = 2
DIFF_TK = 512
DIFF_TQ = 4 * DIFF_TK
FFN_ROWS = 512
OUTPROJ_ROWS = 256
FFN_CHUNK = 512
HALO = SUBLANES


def _alibi_slopes(n):
    def pow2(m):
        start = 2.0 ** (-8.0 / m)
        return [start ** (i + 1) for i in range(m)]
    if math.log2(n).is_integer():
        s = pow2(n)
    else:
        c = 2 ** int(math.floor(math.log2(n)))
        s = pow2(c) + pow2(2 * c)[0::2][: n - c]
    return np.array(sorted(s, reverse=True), dtype=np.float32)


def _rms(xf, g):
    return xf * lax.rsqrt(jnp.mean(xf * xf, axis=-1, keepdims=True) + EPS) * g


def _params(n_axes):
    return pltpu.CompilerParams(
        dimension_semantics=("arbitrary",) * n_axes,
        vmem_limit_bytes=VMEM_LIMIT_BYTES)


def _const_spec(shape):
    nd = len(shape)
    return pl.BlockSpec(shape, lambda *_: (0,) * nd, pipeline_mode=pl.Buffered(1))


def _proj_kernel(x_ref, g_ref, w_ref, wvt_ref, wup_ref, wdn_ref, wout_ref,
                 qa_ref, ka_ref, qb_ref, kb_ref, vat_ref, vbt_ref, wup_bf_ref, wdn_bf_ref, wout_bf_ref):
    wup_bf_ref[...] = wup_ref[...].astype(BF16)
    wdn_bf_ref[...] = wdn_ref[...].astype(BF16)
    wout_bf_ref[...] = wout_ref[...].astype(BF16)

    h = _rms(x_ref[...], g_ref[...]).astype(BF16)
    y = jnp.dot(h, w_ref[...], preferred_element_type=F32)
    scale = HEAD_DIM ** -0.5 * LOG2E
    o = 0
    for ref, sc in ((qa_ref, scale), (ka_ref, None), (qb_ref, scale), (kb_ref, None)):
        w = ref.shape[-1]
        piece = y[:, o:o + w]
        if sc is not None:
            piece = piece * sc
        ref[...] = piece.astype(BF16)
        o += w
    vt = lax.dot_general(wvt_ref[...], h, (((1,), (1,)), ((), ())),
                         preferred_element_type=F32).astype(BF16)
    va_rows = vat_ref.shape[1]
    vat_ref[0] = vt[:va_rows]
    tk = vbt_ref.shape[-1]
    for r in range(vbt_ref.shape[1]):
        vbt_ref[0, r] = vt[va_rows:, r * tk:(r + 1) * tk]


def _proj(x2, g, w_main, w_vt, w_up, w_down, w_out, batch, seq):
    t, d = x2.shape
    tm = PROJ_IN_ROWS
    tk = DIFF_TK
    ns = seq // tm
    nsteps = t // tm
    up_cols = w_up.shape[1] // nsteps
    dn_rows = w_down.shape[0] // nsteps
    out_rows = w_out.shape[0] // nsteps
    assert up_cols % LANES == 0 and dn_rows % BF16_ROWS == 0 and out_rows % BF16_ROWS == 0
    cast_specs = [pl.BlockSpec((w_up.shape[0], up_cols), lambda i: (0, i)),
                  pl.BlockSpec((dn_rows, w_down.shape[1]), lambda i: (i, 0)),
                  pl.BlockSpec((out_rows, w_out.shape[1]), lambda i: (i, 0))]
    qa_w = SWA_Q_HEADS * HEAD_DIM
    ka_w = SWA_KV_HEADS * HEAD_DIM
    qb_w = DIFF_HEADS * 2 * HEAD_DIM
    vb_w = DIFF_HEADS * DIFF_V_DIM
    widths = (qa_w, ka_w, qb_w, qb_w)
    out_shape = [jax.ShapeDtypeStruct((t, w), BF16) for w in widths]
    out_shape.append(jax.ShapeDtypeStruct((batch, ka_w, seq), BF16))
    out_shape.append(jax.ShapeDtypeStruct((batch, seq // tk, vb_w, tk), BF16))
    out_specs = [pl.BlockSpec((tm, w), lambda i: (i, 0)) for w in widths]
    out_specs.append(pl.BlockSpec((1, ka_w, tm), lambda i: (i // ns, 0, i % ns)))
    out_specs.append(pl.BlockSpec((1, tm // tk, vb_w, tk), lambda i: (i // ns, i % ns, 0, 0)))
    out_shape.extend(jax.ShapeDtypeStruct(w.shape, BF16) for w in (w_up, w_down, w_out))
    out_specs.extend(cast_specs)
    return pl.pallas_call(
        _proj_kernel,
        grid=(nsteps,),
        in_specs=[pl.BlockSpec((tm, d), lambda i: (i, 0)),
                  _const_spec(g.shape), _const_spec(w_main.shape), _const_spec(w_vt.shape)]
                 + cast_specs,
        out_specs=out_specs,
        out_shape=out_shape,
        compiler_params=_params(1),
        name="proj_in",
    )(x2, g, w_main, w_vt, w_up, w_down, w_out)


def _swa_q_columns():
    cols = []
    for t in range(SWA_GROUP):
        for g in range(SWA_KV_HEADS):
            head = g * SWA_GROUP + t
            cols.extend(range(head * HEAD_DIM, (head + 1) * HEAD_DIM))
    return np.array(cols, dtype=np.int32)


def _swa_bias(slopes):
    qpos = np.arange(WINDOW)[None, :] + WINDOW
    kpos = np.arange(2 * WINDOW)[:, None]
    dist = (qpos - kpos).astype(np.float32)
    valid = (dist >= 0) & (dist < WINDOW)
    first_valid = valid & (kpos >= WINDOW)
    out = np.empty((2, SWA_KV_HEADS, 2 * WINDOW, SWA_GROUP * WINDOW), np.float32)
    for g in range(SWA_KV_HEADS):
        for t in range(SWA_GROUP):
            sl = np.float32(slopes[g * SWA_GROUP + t] * LOG2E)
            cols = slice(t * WINDOW, (t + 1) * WINDOW)
            out[0, g, :, cols] = np.where(valid, -sl * dist, NEG)
            out[1, g, :, cols] = np.where(first_valid, -sl * dist, NEG)
    return out


def _swa_kernel(q_ref, k_ref, kp_ref, vt_ref, vtp_ref, bias_ref, sink_ref, g_ref, o_ref):
    i = pl.program_id(1)
    nblk = q_ref.shape[1] // WINDOW
    lane = lax.broadcasted_iota(jnp.int32, (WINDOW, LANES), 1)
    ones_rows = jnp.ones((DEN_ROWS, 2 * WINDOW), BF16)

    def scores(n, g):
        rows = slice(n * WINDOW, (n + 1) * WINDOW)
        if n == 0:
            kk = jnp.concatenate([kp_ref[0], k_ref[0, rows, :]], axis=0)
        else:
            kk = k_ref[0, (n - 1) * WINDOW:(n + 1) * WINDOW, :]
        keep = (lane < HEAD_DIM) if g == 0 else (lane >= HEAD_DIM)
        q_tiles = [q_ref[0, rows, t * LANES:(t + 1) * LANES] for t in range(SWA_GROUP)]
        qm = jnp.concatenate([jnp.where(keep, qt, jnp.zeros_like(qt)) for qt in q_tiles],
                             axis=0)
        s = lax.dot_general(kk, qm, (((1,), (1,)), ((), ())),
                            preferred_element_type=F32)
        if n == 0:
            bias = jnp.where(i == 0, bias_ref[1, g], bias_ref[0, g])
        else:
            bias = bias_ref[0, g]
        return s + bias

    def weighted_values(n, g, s):
        if n == 0:
            vvt = jnp.concatenate([vtp_ref[0], vt_ref[0, :, 0:WINDOW]], axis=1)
        else:
            vvt = vt_ref[0, :, (n - 1) * WINDOW:(n + 1) * WINDOW]
        sink = sink_ref[g]
        m = jnp.maximum(jnp.max(s, axis=0, keepdims=True), sink)
        p = jnp.exp2(s - m).astype(BF16)
        vx = jnp.concatenate([vvt[g * HEAD_DIM:(g + 1) * HEAD_DIM], ones_rows], axis=0)
        o = jnp.dot(vx, p, preferred_element_type=F32)
        l = o[HEAD_DIM:HEAD_DIM + 1] + jnp.exp2(sink - m)
        ot = o[0:HEAD_DIM] * (1.0 / l)
        return [ot[:, t * WINDOW:(t + 1) * WINDOW] for t in range(SWA_GROUP)]

    tasks = [(n, g) for n in range(nblk) for g in range(SWA_KV_HEADS)]
    pending = [scores(*t) for t in tasks[:SWA_LOOKAHEAD]]
    pieces = []
    for idx, (n, g) in enumerate(tasks):
        s_cur = pending.pop(0)
        if idx + SWA_LOOKAHEAD < len(tasks):
            pending.append(scores(*tasks[idx + SWA_LOOKAHEAD]))
        pieces.extend(weighted_values(n, g, s_cur))
        if g == SWA_KV_HEADS - 1:
            y = jnp.concatenate(pieces, axis=0).T
            o_ref[0, n * WINDOW:(n + 1) * WINDOW, :] = _rms(y, g_ref[...]).astype(BF16)
            pieces = []


def _swa(qa, ka, vat, bias, sink_rows, g):
    b, s, qw = qa.shape
    kw = ka.shape[-1]
    tq = SWA_ROWS
    r = tq // WINDOW
    cur = lambda bi, i: (bi, i, 0)
    prev = lambda bi, i: (bi, jnp.maximum(i * r - 1, 0), 0)
    cur_t = lambda bi, i: (bi, 0, i)
    prev_t = lambda bi, i: (bi, 0, jnp.maximum(i * r - 1, 0))
    return pl.pallas_call(
        _swa_kernel,
        grid=(b, s // tq),
        in_specs=[pl.BlockSpec((1, tq, qw), cur),
                  pl.BlockSpec((1, tq, kw), cur), pl.BlockSpec((1, WINDOW, kw), prev),
                  pl.BlockSpec((1, kw, tq), cur_t), pl.BlockSpec((1, kw, WINDOW), prev_t),
                  _const_spec(bias.shape), _const_spec(sink_rows.shape), _const_spec(g.shape)],
        out_specs=pl.BlockSpec((1, tq, qw), cur),
        out_shape=jax.ShapeDtypeStruct((b, s, qw), BF16),
        compiler_params=_params(2),
        name="swa_attn",
    )(qa, ka, ka, vat, vat, bias, sink_rows, g)


N_BIAS_COLS = 3
M_INIT = -1e28
DEN_LIMIT = 2.0 ** 60


def _diff_kernel(slopes_ref, q_ref, k_ref, vt_ref, lq1_ref, lk1_ref, lq2_ref, lk2_ref, g_ref,
                 o_ref, qx_ref, kf_ref, pa_ref, pb_ref, aa_ref, ab_ref, acc_ref, m_ref, l_ref,
                 *, lambda_init):
    h = pl.program_id(1)
    i = pl.program_id(2)
    tq = q_ref.shape[1]
    tk = kf_ref.shape[0]
    w = q_ref.shape[2]
    slope2 = slopes_ref[h] * LOG2E
    ndiag = tq // tk
    assert tq == ndiag * tk and ndiag % 2 == 0

    @pl.when(i == 0)
    def _():
        kloc = lax.broadcasted_iota(jnp.int32, (tk, w), 0).astype(F32)
        col = lax.broadcasted_iota(jnp.int32, (tk, w), 1)
        rest = slope2 * kloc
        feat = jnp.zeros((tk, w), F32)
        for n in range(N_BIAS_COLS):
            part = rest.astype(BF16).astype(F32)
            feat = jnp.where(col == n, part, feat)
            rest = rest - part
        kf_ref[...] = feat.astype(BF16)

    q = q_ref[0]
    lane = lax.broadcasted_iota(jnp.int32, q.shape, 1)
    zero = jnp.zeros_like(q)
    ones_cols = jnp.where(lane < N_BIAS_COLS, 1.0, 0.0).astype(BF16)
    qx_ref[0] = jnp.concatenate([jnp.where(lane < HEAD_DIM, q, zero), ones_cols], axis=1)
    qx_ref[1] = jnp.concatenate([jnp.where(lane >= HEAD_DIM, q, zero), ones_cols], axis=1)
    def softmax(j, p_ref, alpha_ref, lo, masked, track_max):
        nq = tq - lo
        kt = k_ref[0, pl.ds(pl.multiple_of(j * tk, tk), tk), :]
        kx = jnp.concatenate([kt, kf_ref[...]], axis=1)
        c = slope2 * (j * tk).astype(F32)
        for mp in range(2):
            s = lax.dot_general(kx, qx_ref[mp, lo:tq, :], (((1,), (1,)), ((), ())),
                                preferred_element_type=F32)
            if masked:
                krow = lax.broadcasted_iota(jnp.int32, (tk, nq), 0)
                qcol = lax.broadcasted_iota(jnp.int32, (tk, nq), 1)
                s = jnp.where(krow <= qcol, s, NEG)
            m = m_ref[mp, :, lo:tq]
            if track_max:
                m_old = m
                m = jnp.maximum(m_old, jnp.max(s, axis=0, keepdims=True) + c)
                alpha_ref[mp, :, lo:tq] = jnp.exp2(m_old - m)
                m_ref[mp, :, lo:tq] = m
            p = jnp.exp2(s - (m - c))
            p_ref[mp, :, lo:tq] = p.astype(BF16)
            psum = jnp.sum(p, axis=0, keepdims=True)
            if track_max:
                l_ref[mp, :, lo:tq] = alpha_ref[mp, :, lo:tq] * l_ref[mp, :, lo:tq] + psum
            else:
                l_ref[mp, :, lo:tq] += psum

    def weighted_values(j, p_ref, alpha_ref, lo, rescale):
        vt = vt_ref[0, j]
        for mp in range(2):
            pv = jnp.dot(vt, p_ref[mp, :, lo:tq], preferred_element_type=F32)
            if rescale:
                acc_ref[mp, :, lo:tq] = alpha_ref[mp, :, lo:tq] * acc_ref[mp, :, lo:tq] + pv
            else:
                acc_ref[mp, :, lo:tq] += pv

    def attend(track_max):
        acc_ref[...] = jnp.zeros_like(acc_ref)
        l_ref[...] = jnp.zeros_like(l_ref)
        bufs = ((pa_ref, aa_ref), (pb_ref, ab_ref))
        first = ndiag * i
        for step, d in enumerate(range(ndiag - 1, -1, -1)):
            softmax(first + d, *bufs[step % 2], d * tk, True, track_max)
            if step > 0:
                weighted_values(first + d + 1, *bufs[(step - 1) % 2], (d + 1) * tk, track_max)

        def pair(t, carry):
            j = first - 1 - 2 * t
            softmax(j, pa_ref, aa_ref, 0, False, track_max)
            weighted_values(j + 1, pb_ref, ab_ref, 0, track_max)
            softmax(j - 1, pb_ref, ab_ref, 0, False, track_max)
            weighted_values(j, pa_ref, aa_ref, 0, track_max)
            return carry

        lax.fori_loop(0, first // 2, pair, 0)

    k_self = k_ref[0, pl.ds(pl.multiple_of(i * tq, tq), tq), :]
    qk = (q.astype(F32) * k_self.astype(F32)).astype(BF16)
    half = lax.broadcasted_iota(jnp.int32, (BF16_ROWS, w), 1) < HEAD_DIM
    qpos = (i * tq + lax.broadcasted_iota(jnp.int32, (1, tq), 1)).astype(F32)
    for mp in range(2):
        pick = jnp.where(half if mp == 0 else jnp.logical_not(half), 1.0, 0.0).astype(BF16)
        self_score = lax.dot_general(pick, qk, (((1,), (1,)), ((), ())),
                                     preferred_element_type=F32)[0:1]
        m_ref[mp] = self_score + slope2 * qpos
    ab_ref[...] = jnp.ones_like(ab_ref)
    attend(False)
    den_max = jnp.max(jnp.maximum(l_ref[0], l_ref[1]))

    @pl.when(jnp.logical_not(den_max <= DEN_LIMIT))
    def _():
        m_ref[...] = jnp.full_like(m_ref, M_INIT)
        attend(True)

    weighted_values(0, pb_ref, ab_ref, 0, True)

    lam =(jnp.exp(jnp.sum(lq1_ref[...] * lk1_ref[...], axis=-1, keepdims=True))
           - jnp.exp(jnp.sum(lq2_ref[...] * lk2_ref[...], axis=-1, keepdims=True))
           + lambda_init)
    y = acc_ref[0] * (1.0 / l_ref[0]) - acc_ref[1] * (lam / l_ref[1])
    ynorm = y * lax.rsqrt(jnp.mean(y * y, axis=0, keepdims=True) + EPS) * g_ref[...]
    o_ref[0] = (ynorm * (1.0 - lambda_init)).astype(BF16)


def _diff(qb, kb, vbt, slopes, lq1, lk1, lq2, lk2, g, lambda_init):
    b, s, _ = qb.shape
    tq, tk = DIFF_TQ, DIFF_TK
    w = DIFF_V_DIM
    ntk = vbt.shape[1]
    grid_spec = pltpu.PrefetchScalarGridSpec(
        num_scalar_prefetch=1,
        grid=(b, DIFF_HEADS, s // tq),
        in_specs=[pl.BlockSpec((1, tq, w), lambda bi, h, i, sl: (bi, i, h)),
                  pl.BlockSpec((1, s, w), lambda bi, h, i, sl: (bi, 0, h)),
                  pl.BlockSpec((1, ntk, w, tk), lambda bi, h, i, sl: (bi, 0, h, 0)),
                  _const_spec(lq1.shape), _const_spec(lk1.shape),
                  _const_spec(lq2.shape), _const_spec(lk2.shape), _const_spec(g.shape)],
        out_specs=pl.BlockSpec((1, w, tq), lambda bi, h, i, sl: (bi, h, i)),
        scratch_shapes=[pltpu.VMEM((2, tq, 2 * w), BF16),
                        pltpu.VMEM((tk, w), BF16),
                        pltpu.VMEM((2, tk, tq), BF16),
                        pltpu.VMEM((2, tk, tq), BF16),
                        pltpu.VMEM((2, 1, tq), F32),
                        pltpu.VMEM((2, 1, tq), F32),
                        pltpu.VMEM((2, w, tq), F32),
                        pltpu.VMEM((2, 1, tq), F32),
                        pltpu.VMEM((2, 1, tq), F32)])
    return pl.pallas_call(
        functools.partial(_diff_kernel, lambda_init=lambda_init),
        grid_spec=grid_spec,
        out_shape=jax.ShapeDtypeStruct((b, DIFF_HEADS * w, s), BF16),
        compiler_params=_params(3),
        name="diff_attn",
    )(slopes, qb, kb, vbt, lq1, lk1, lq2, lk2, g)


def _gelu_tanh_gate(x, val):
    k0 = math.sqrt(2.0 / math.pi)
    inner = x * (k0 + (k0 * 0.044715) * (x * x))
    hv = (0.5 * x) * val
    return hv + hv * jnp.tanh(inner)


def _ffn_kernel(ya_ref, ybt_ref, x_ref, wout_ref, gatt_ref, gpre_ref, wup_ref, cw_ref, wd_ref,
                gpost_ref, o_ref, x1_ref, h_ref, u_ref, tail_ref, acta_ref, actb_ref, acc_ref):
    i = pl.program_id(1)
    ts = x_ref.shape[1]
    d_ff = wd_ref.shape[0]
    fc = acta_ref.shape[-1]
    nlt = fc // LANES
    nchunk = d_ff // fc
    assert nchunk % 2 == 0

    @pl.when(i == 0)
    def _():
        tail_ref[...] = jnp.zeros_like(tail_ref)

    ka = ya_ref.shape[-1]
    for r in range(0, ts, OUTPROJ_ROWS):
        rows = slice(r, r + OUTPROJ_ROWS)
        mix = jnp.dot(ya_ref[0, rows, :], wout_ref[0:ka, :], preferred_element_type=F32)
        mix = mix + lax.dot_general(ybt_ref[0, :, rows], wout_ref[ka:, :],
                                    (((0,), (0,)), ((), ())), preferred_element_type=F32)
        x1 = x_ref[0, rows, :] + _rms(mix, gatt_ref[...])
        x1_ref[rows, :] = x1
        h_ref[rows, :] = _rms(x1, gpre_ref[...]).astype(BF16)

    def cols(c, half):
        return pl.ds(pl.multiple_of(half * d_ff + c * fc, fc), fc)

    def up_act(c, act_ref):
        hh = h_ref[...]
        outs = []
        for half in range(2):
            u = jnp.dot(hh, wup_ref[:, cols(c, half)], preferred_element_type=F32)
            for lt in range(nlt):
                slab = half * nlt + lt
                piece = u[:, lt * LANES:(lt + 1) * LANES]
                u_ref[slab, 0:HALO] = tail_ref[c * (2 * nlt) + slab]
                u_ref[slab, HALO:] = piece
                tail_ref[c * (2 * nlt) + slab] = piece[ts - HALO:]
            cp = cw_ref[:, cols(c, half)]
            out = cp[CONV_WIDTH:CONV_WIDTH + 1]
            for tap in range(CONV_WIDTH):
                start = HALO - (CONV_WIDTH - 1) + tap
                shifted = jnp.concatenate(
                    [u_ref[pl.ds(half * nlt + lt, 1, stride=2), pl.ds(start, ts), :].reshape(ts, LANES)
                     for lt in range(nlt)], axis=1)
                out = out + cp[tap:tap + 1] * shifted
            outs.append(out)
        act_ref[...] = _gelu_tanh_gate(outs[0], outs[1]).astype(BF16)

    def down(c, act_ref):
        rows = pl.ds(pl.multiple_of(c * fc, fc), fc)
        acc_ref[...] += jnp.dot(act_ref[...], wd_ref[rows, :], preferred_element_type=F32)

    acc_ref[...] = jnp.zeros_like(acc_ref)
    up_act(0, acta_ref)

    def body(t, carry):
        c = 2 * t + 1
        up_act(c, actb_ref)
        down(c - 1, acta_ref)
        up_act(c + 1, acta_ref)
        down(c, actb_ref)
        return carry

    lax.fori_loop(0, (nchunk - 2) // 2, body, 0)
    up_act(nchunk - 1, actb_ref)
    down(nchunk - 2, acta_ref)
    down(nchunk - 1, actb_ref)

    o_ref[0] = x1_ref[...] + _rms(acc_ref[...], gpost_ref[...])


def _outproj_ffn(ya, ybt, x3, w_out, gatt, gpre, w_up, cw, w_down, gpost):
    b, s, d = x3.shape
    ts = FFN_ROWS
    fc = FFN_CHUNK
    nslab = 2 * fc // LANES
    nchunk = w_down.shape[0] // fc
    consts = (w_out, gatt, gpre, w_up, cw, w_down, gpost)
    return pl.pallas_call(
        _ffn_kernel,
        grid=(b, s // ts),
        in_specs=[pl.BlockSpec((1, ts, ya.shape[-1]), lambda bi, i: (bi, i, 0)),
                  pl.BlockSpec((1, ybt.shape[1], ts), lambda bi, i: (bi, 0, i)),
                  pl.BlockSpec((1, ts, d), lambda bi, i: (bi, i, 0))]
                 + [_const_spec(c.shape) for c in consts],
        out_specs=pl.BlockSpec((1, ts, d), lambda bi, i: (bi, i, 0)),
        out_shape=jax.ShapeDtypeStruct((b, s, d), F32),
        scratch_shapes=[pltpu.VMEM((ts, d), F32),
                        pltpu.VMEM((ts, d), BF16),
                        pltpu.VMEM((nslab, HALO + ts, LANES), F32),
                        pltpu.VMEM((nchunk * nslab, HALO, LANES), F32),
                        pltpu.VMEM((ts, fc), BF16),
                        pltpu.VMEM((ts, fc), BF16),
                        pltpu.VMEM((ts, d), F32)],
        compiler_params=_params(2),
        name="outproj_ffn",
    )(ya, ybt, x3, *consts)


def kernel(x, attn_pre_g, w_in, swa_sinks, swa_out_g, diff_lq1, diff_lk1, diff_lq2, diff_lk2,
           diff_subln_g, w_out, attn_post_g, ffn_pre_g, w_up, conv_w, conv_b, w_down, ffn_post_g):
    batch, seq, d = x.shape
    depth = w_in.shape[0]
    slopes = _alibi_slopes(SWA_Q_HEADS + DIFF_HEADS)
    swa_bias = jnp.asarray(_swa_bias(slopes[:SWA_Q_HEADS]))
    diff_slopes = jnp.asarray(slopes[SWA_Q_HEADS:])
    qa_cols = _swa_q_columns()
    row = lambda v: v.reshape(1, -1).astype(F32)
    qa_w = SWA_Q_HEADS * HEAD_DIM
    kv_w = SWA_KV_HEADS * HEAD_DIM
    qb_w = DIFF_HEADS * 2 * HEAD_DIM
    va0 = qa_w + kv_w
    qb0 = va0 + kv_w
    vb0 = qb0 + 2 * qb_w

    for layer in range(depth):
        lambda_init = 0.8 - 0.6 * math.exp(-0.3 * layer)
        wl = w_in[layer]
        w_main = jnp.concatenate(
            [wl[:, qa_cols], wl[:, qa_w:va0], wl[:, qb0:vb0]], axis=1).astype(BF16)
        w_vt = jnp.concatenate([wl[:, va0:qb0], wl[:, vb0:]], axis=1).T.astype(BF16)
        qa, ka, qb, kb, vat, vbt, w_up_bf, w_down_bf, w_out_bf = _proj(
            x.reshape(batch * seq, d), row(attn_pre_g[layer]), w_main, w_vt,
            w_up[layer], w_down[layer], w_out[layer], batch, seq)
        sink_rows = jnp.repeat(swa_sinks[layer].astype(F32) * LOG2E, WINDOW).reshape(
            SWA_KV_HEADS, 1, SWA_GROUP * WINDOW)
        ya = _swa(qa.reshape(batch, seq, -1), ka.reshape(batch, seq, -1), vat,
                  swa_bias, sink_rows, row(swa_out_g[layer]))
        yb = _diff(qb.reshape(batch, seq, -1), kb.reshape(batch, seq, -1), vbt, diff_slopes,
                   row(diff_lq1[layer]), row(diff_lk1[layer]), row(diff_lq2[layer]),
                   row(diff_lk2[layer]), diff_subln_g[layer].reshape(-1, 1).astype(F32), lambda_init)
        cw = jnp.concatenate([conv_w[layer], conv_b[layer][None, :]], axis=0).astype(F32)
        cw = jnp.pad(cw, ((0, SUBLANES - cw.shape[0]), (0, 0)))
        x = _outproj_ffn(ya, yb, x, w_out_bf, row(attn_post_g[layer]), row(ffn_pre_g[layer]),
                         w_up_bf, cw, w_down_bf, row(ffn_post_g[layer]))
    return x
```

```python
import functools
import math

import numpy as np
import jax
import jax.numpy as jnp
from jax import lax
from jax.experimental import pallas as pl
from jax.experimental.pallas import tpu as pltpu

F32 = jnp.float32
BF16 = jnp.bfloat16

EPS = 1e-6
HEAD_DIM = 64
SWA_Q_HEADS = 8
SWA_KV_HEADS = 2
SWA_GROUP = SWA_Q_HEADS // SWA_KV_HEADS
WINDOW = 128
DIFF_HEADS = 4
DIFF_V_DIM = 2 * HEAD_DIM
CONV_WIDTH = 3
NEG = -1e30
LOG2E = math.log2(math.e)

LANES = 128
SUBLANES = 8
BF16_ROWS = 16
DEN_ROWS = BF16_ROWS
VMEM_LIMIT_BYTES = 56 * 1024 * 1024

PROJ_IN_ROWS = 1024
SWA_ROWS = 2048
SWA_LOOKAHEAD = 2
DIFF_TK = 512
DIFF_TQ = 4 * DIFF_TK
FFN_ROWS = 512
OUTPROJ_ROWS = 256
FFN_CHUNK = 512
HALO = SUBLANES


def _alibi_slopes(n):
    def pow2(m):
        start = 2.0 ** (-8.0 / m)
        return [start ** (i + 1) for i in range(m)]
    if math.log2(n).is_integer():
        s = pow2(n)
    else:
        c = 2 ** int(math.floor(math.log2(n)))
        s = pow2(c) + pow2(2 * c)[0::2][: n - c]
    return np.array(sorted(s, reverse=True), dtype=np.float32)


def _rms(xf, g):
    return xf * lax.rsqrt(jnp.mean(xf * xf, axis=-1, keepdims=True) + EPS) * g


def _params(n_axes):
    return pltpu.CompilerParams(
        dimension_semantics=("arbitrary",) * n_axes,
        vmem_limit_bytes=VMEM_LIMIT_BYTES)


def _const_spec(shape):
    nd = len(shape)
    return pl.BlockSpec(shape, lambda *_: (0,) * nd, pipeline_mode=pl.Buffered(1))


def _proj_kernel(x_ref, g_ref, w_ref, wvt_ref, wup_ref, wdn_ref, wout_ref,
                 qa_ref, ka_ref, qb_ref, kb_ref, vat_ref, vbt_ref, wup_bf_ref, wdn_bf_ref, wout_bf_ref):
    wup_bf_ref[...] = wup_ref[...].astype(BF16)
    wdn_bf_ref[...] = wdn_ref[...].astype(BF16)
    wout_bf_ref[...] = wout_ref[...].astype(BF16)

    h = _rms(x_ref[...], g_ref[...]).astype(BF16)
    y = jnp.dot(h, w_ref[...], preferred_element_type=F32)
    scale = HEAD_DIM ** -0.5 * LOG2E
    o = 0
    for ref, sc in ((qa_ref, scale), (ka_ref, None), (qb_ref, scale), (kb_ref, None)):
        w = ref.shape[-1]
        piece = y[:, o:o + w]
        if sc is not None:
            piece = piece * sc
        ref[...] = piece.astype(BF16)
        o += w
    vt = lax.dot_general(wvt_ref[...], h, (((1,), (1,)), ((), ())),
                         preferred_element_type=F32).astype(BF16)
    va_rows = vat_ref.shape[1]
    vat_ref[0] = vt[:va_rows]
    tk = vbt_ref.shape[-1]
    for r in range(vbt_ref.shape[1]):
        vbt_ref[0, r] = vt[va_rows:, r * tk:(r + 1) * tk]


def _proj(x2, g, w_main, w_vt, w_up, w_down, w_out, batch, seq):
    t, d = x2.shape
    tm = PROJ_IN_ROWS
    tk = DIFF_TK
    ns = seq // tm
    nsteps = t // tm
    up_cols = w_up.shape[1] // nsteps
    dn_rows = w_down.shape[0] // nsteps
    out_rows = w_out.shape[0] // nsteps
    assert up_cols % LANES == 0 and dn_rows % BF16_ROWS == 0 and out_rows % BF16_ROWS == 0
    cast_specs = [pl.BlockSpec((w_up.shape[0], up_cols), lambda i: (0, i)),
                  pl.BlockSpec((dn_rows, w_down.shape[1]), lambda i: (i, 0)),
                  pl.BlockSpec((out_rows, w_out.shape[1]), lambda i: (i, 0))]
    qa_w = SWA_Q_HEADS * HEAD_DIM
    ka_w = SWA_KV_HEADS * HEAD_DIM
    qb_w = DIFF_HEADS * 2 * HEAD_DIM
    vb_w = DIFF_HEADS * DIFF_V_DIM
    widths = (qa_w, ka_w, qb_w, qb_w)
    out_shape = [jax.ShapeDtypeStruct((t, w), BF16) for w in widths]
    out_shape.append(jax.ShapeDtypeStruct((batch, ka_w, seq), BF16))
    out_shape.append(jax.ShapeDtypeStruct((batch, seq // tk, vb_w, tk), BF16))
    out_specs = [pl.BlockSpec((tm, w), lambda i: (i, 0)) for w in widths]
    out_specs.append(pl.BlockSpec((1, ka_w, tm), lambda i: (i // ns, 0, i % ns)))
    out_specs.append(pl.BlockSpec((1, tm // tk, vb_w, tk), lambda i: (i // ns, i % ns, 0, 0)))
    out_shape.extend(jax.ShapeDtypeStruct(w.shape, BF16) for w in (w_up, w_down, w_out))
    out_specs.extend(cast_specs)
    return pl.pallas_call(
        _proj_kernel,
        grid=(nsteps,),
        in_specs=[pl.BlockSpec((tm, d), lambda i: (i, 0)),
                  _const_spec(g.shape), _const_spec(w_main.shape), _const_spec(w_vt.shape)]
                 + cast_specs,
        out_specs=out_specs,
        out_shape=out_shape,
        compiler_params=_params(1),
        name="proj_in",
    )(x2, g, w_main, w_vt, w_up, w_down, w_out)


def _swa_q_columns():
    cols = []
    for t in range(SWA_GROUP):
        for g in range(SWA_KV_HEADS):
            head = g * SWA_GROUP + t
            cols.extend(range(head * HEAD_DIM, (head + 1) * HEAD_DIM))
    return np.array(cols, dtype=np.int32)


def _swa_bias(slopes):
    qpos = np.arange(WINDOW)[None, :] + WINDOW
    kpos = np.arange(2 * WINDOW)[:, None]
    dist = (qpos - kpos).astype(np.float32)
    valid = (dist >= 0) & (dist < WINDOW)
    first_valid = valid & (kpos >= WINDOW)
    out = np.empty((2, SWA_KV_HEADS, 2 * WINDOW, SWA_GROUP * WINDOW), np.float32)
    for g in range(SWA_KV_HEADS):
        for t in range(SWA_GROUP):
            sl = np.float32(slopes[g * SWA_GROUP + t] * LOG2E)
            cols = slice(t * WINDOW, (t + 1) * WINDOW)
            out[0, g, :, cols] = np.where(valid, -sl * dist, NEG)
            out[1, g, :, cols] = np.where(first_valid, -sl * dist, NEG)
    return out


def _swa_kernel(q_ref, k_ref, kp_ref, vt_ref, vtp_ref, bias_ref, sink_ref, g_ref, o_ref):
    i = pl.program_id(1)
    nblk = q_ref.shape[1] // WINDOW
    lane = lax.broadcasted_iota(jnp.int32, (WINDOW, LANES), 1)
    ones_rows = jnp.ones((DEN_ROWS, 2 * WINDOW), BF16)

    def scores(n, g):
        rows = slice(n * WINDOW, (n + 1) * WINDOW)
        if n == 0:
            kk = jnp.concatenate([kp_ref[0], k_ref[0, rows, :]], axis=0)
        else:
            kk = k_ref[0, (n - 1) * WINDOW:(n + 1) * WINDOW, :]
        keep = (lane < HEAD_DIM) if g == 0 else (lane >= HEAD_DIM)
        q_tiles = [q_ref[0, rows, t * LANES:(t + 1) * LANES] for t in range(SWA_GROUP)]
        qm = jnp.concatenate([jnp.where(keep, qt, jnp.zeros_like(qt)) for qt in q_tiles],
                             axis=0)
        s = lax.dot_general(kk, qm, (((1,), (1,)), ((), ())),
                            preferred_element_type=F32)
        if n == 0:
            bias = jnp.where(i == 0, bias_ref[1, g], bias_ref[0, g])
        else:
            bias = bias_ref[0, g]
        return s + bias

    def weighted_values(n, g, s):
        if n == 0:
            vvt = jnp.concatenate([vtp_ref[0], vt_ref[0, :, 0:WINDOW]], axis=1)
        else:
            vvt = vt_ref[0, :, (n - 1) * WINDOW:(n + 1) * WINDOW]
        sink = sink_ref[g]
        m = jnp.maximum(jnp.max(s, axis=0, keepdims=True), sink)
        p = jnp.exp2(s - m).astype(BF16)
        vx = jnp.concatenate([vvt[g * HEAD_DIM:(g + 1) * HEAD_DIM], ones_rows], axis=0)
        o = jnp.dot(vx, p, preferred_element_type=F32)
        l = o[HEAD_DIM:HEAD_DIM + 1] + jnp.exp2(sink - m)
        ot = o[0:HEAD_DIM] * (1.0 / l)
        return [ot[:, t * WINDOW:(t + 1) * WINDOW] for t in range(SWA_GROUP)]

    tasks = [(n, g) for n in range(nblk) for g in range(SWA_KV_HEADS)]
    pending = [scores(*t) for t in tasks[:SWA_LOOKAHEAD]]
    pieces = []
    for idx, (n, g) in enumerate(tasks):
        s_cur = pending.pop(0)
        if idx + SWA_LOOKAHEAD < len(tasks):
            pending.append(scores(*tasks[idx + SWA_LOOKAHEAD]))
        pieces.extend(weighted_values(n, g, s_cur))
        if g == SWA_KV_HEADS - 1:
            y = jnp.concatenate(pieces, axis=0).T
            o_ref[0, n * WINDOW:(n + 1) * WINDOW, :] = _rms(y, g_ref[...]).astype(BF16)
            pieces = []


def _swa(qa, ka, vat, bias, sink_rows, g):
    b, s, qw = qa.shape
    kw = ka.shape[-1]
    tq = SWA_ROWS
    r = tq // WINDOW
    cur = lambda bi, i: (bi, i, 0)
    prev = lambda bi, i: (bi, jnp.maximum(i * r - 1, 0), 0)
    cur_t = lambda bi, i: (bi, 0, i)
    prev_t = lambda bi, i: (bi, 0, jnp.maximum(i * r - 1, 0))
    return pl.pallas_call(
        _swa_kernel,
        grid=(b, s // tq),
        in_specs=[pl.BlockSpec((1, tq, qw), cur),
                  pl.BlockSpec((1, tq, kw), cur), pl.BlockSpec((1, WINDOW, kw), prev),
                  pl.BlockSpec((1, kw, tq), cur_t), pl.BlockSpec((1, kw, WINDOW), prev_t),
                  _const_spec(bias.shape), _const_spec(sink_rows.shape), _const_spec(g.shape)],
        out_specs=pl.BlockSpec((1, tq, qw), cur),
        out_shape=jax.ShapeDtypeStruct((b, s, qw), BF16),
        compiler_params=_params(2),
        name="swa_attn",
    )(qa, ka, ka, vat, vat, bias, sink_rows, g)


N_BIAS_COLS = 3
M_INIT = -1e28
DEN_LIMIT = 2.0 ** 60


def _diff_kernel(slopes_ref, q_ref, k_ref, vt_ref, lq1_ref, lk1_ref, lq2_ref, lk2_ref, g_ref,
                 o_ref, qx_ref, kf_ref, pa_ref, pb_ref, aa_ref, ab_ref, acc_ref, m_ref, l_ref,
                 *, lambda_init):
    h = pl.program_id(1)
    i = pl.program_id(2)
    tq = q_ref.shape[1]
    tk = kf_ref.shape[0]
    w = q_ref.shape[2]
    slope2 = slopes_ref[h] * LOG2E
    ndiag = tq // tk
    assert tq == ndiag * tk and ndiag % 2 == 0

    @pl.when(i == 0)
    def _():
        kloc = lax.broadcasted_iota(jnp.int32, (tk, w), 0).astype(F32)
        col = lax.broadcasted_iota(jnp.int32, (tk, w), 1)
        rest = slope2 * kloc
        feat = jnp.zeros((tk, w), F32)
        for n in range(N_BIAS_COLS):
            part = rest.astype(BF16).astype(F32)
            feat = jnp.where(col == n, part, feat)
            rest = rest - part
        kf_ref[...] = feat.astype(BF16)

    q = q_ref[0]
    lane = lax.broadcasted_iota(jnp.int32, q.shape, 1)
    zero = jnp.zeros_like(q)
    ones_cols = jnp.where(lane < N_BIAS_COLS, 1.0, 0.0).astype(BF16)
    qx_ref[0] = jnp.concatenate([jnp.where(lane < HEAD_DIM, q, zero), ones_cols], axis=1)
    qx_ref[1] = jnp.concatenate([jnp.where(lane >= HEAD_DIM, q, zero), ones_cols], axis=1)
    def softmax(j, p_ref, alpha_ref, lo, masked, track_max):
        nq = tq - lo
        kt = k_ref[0, pl.ds(pl.multiple_of(j * tk, tk), tk), :]
        kx = jnp.concatenate([kt, kf_ref[...]], axis=1)
        c = slope2 * (j * tk).astype(F32)
        for mp in range(2):
            s = lax.dot_general(kx, qx_ref[mp, lo:tq, :], (((1,), (1,)), ((), ())),
                                preferred_element_type=F32)
            if masked:
                krow = lax.broadcasted_iota(jnp.int32, (tk, nq), 0)
                qcol = lax.broadcasted_iota(jnp.int32, (tk, nq), 1)
                s = jnp.where(krow <= qcol, s, NEG)
            m = m_ref[mp, :, lo:tq]
            if track_max:
                m_old = m
                m = jnp.maximum(m_old, jnp.max(s, axis=0, keepdims=True) + c)
                alpha_ref[mp, :, lo:tq] = jnp.exp2(m_old - m)
                m_ref[mp, :, lo:tq] = m
            p = jnp.exp2(s - (m - c))
            p_ref[mp, :, lo:tq] = p.astype(BF16)
            psum = jnp.sum(p, axis=0, keepdims=True)
            if track_max:
                l_ref[mp, :, lo:tq] = alpha_ref[mp, :, lo:tq] * l_ref[mp, :, lo:tq] + psum
            else:
                l_ref[mp, :, lo:tq] += psum

    def weighted_values(j, p_ref, alpha_ref, lo, rescale):
        vt = vt_ref[0, j]
        for mp in range(2):
            pv = jnp.dot(vt, p_ref[mp, :, lo:tq], preferred_element_type=F32)
            if rescale:
                acc_ref[mp, :, lo:tq] = alpha_ref[mp, :, lo:tq] * acc_ref[mp, :, lo:tq] + pv
            else:
                acc_ref[mp, :, lo:tq] += pv

    def attend(track_max):
        acc_ref[...] = jnp.zeros_like(acc_ref)
        l_ref[...] = jnp.zeros_like(l_ref)
        bufs = ((pa_ref, aa_ref), (pb_ref, ab_ref))
        first = ndiag * i
        for step, d in enumerate(range(ndiag - 1, -1, -1)):
            softmax(first + d, *bufs[step % 2], d * tk, True, track_max)
            if step > 0:
                weighted_values(first + d + 1, *bufs[(step - 1) % 2], (d + 1) * tk, track_max)

        def pair(t, carry):
            j = first - 1 - 2 * t
            softmax(j, pa_ref, aa_ref, 0, False, track_max)
            weighted_values(j + 1, pb_ref, ab_ref, 0, track_max)
            softmax(j - 1, pb_ref, ab_ref, 0, False, track_max)
            weighted_values(j, pa_ref, aa_ref, 0, track_max)
            return carry

        lax.fori_loop(0, first // 2, pair, 0)

    k_self = k_ref[0, pl.ds(pl.multiple_of(i * tq, tq), tq), :]
    qk = (q.astype(F32) * k_self.astype(F32)).astype(BF16)
    half = lax.broadcasted_iota(jnp.int32, (BF16_ROWS, w), 1) < HEAD_DIM
    qpos = (i * tq + lax.broadcasted_iota(jnp.int32, (1, tq), 1)).astype(F32)
    for mp in range(2):
        pick = jnp.where(half if mp == 0 else jnp.logical_not(half), 1.0, 0.0).astype(BF16)
        self_score = lax.dot_general(pick, qk, (((1,), (1,)), ((), ())),
                                     preferred_element_type=F32)[0:1]
        m_ref[mp] = self_score + slope2 * qpos
    ab_ref[...] = jnp.ones_like(ab_ref)
    attend(False)
    den_max = jnp.max(jnp.maximum(l_ref[0], l_ref[1]))

    @pl.when(jnp.logical_not(den_max <= DEN_LIMIT))
    def _():
        m_ref[...] = jnp.full_like(m_ref, M_INIT)
        attend(True)

    weighted_values(0, pb_ref, ab_ref, 0, True)

    lam =(jnp.exp(jnp.sum(lq1_ref[...] * lk1_ref[...], axis=-1, keepdims=True))
           - jnp.exp(jnp.sum(lq2_ref[...] * lk2_ref[...], axis=-1, keepdims=True))
           + lambda_init)
    y = acc_ref[0] * (1.0 / l_ref[0]) - acc_ref[1] * (lam / l_ref[1])
    ynorm = y * lax.rsqrt(jnp.mean(y * y, axis=0, keepdims=True) + EPS) * g_ref[...]
    o_ref[0] = (ynorm * (1.0 - lambda_init)).astype(BF16)


def _diff(qb, kb, vbt, slopes, lq1, lk1, lq2, lk2, g, lambda_init):
    b, s, _ = qb.shape
    tq, tk = DIFF_TQ, DIFF_TK
    w = DIFF_V_DIM
    ntk = vbt.shape[1]
    grid_spec = pltpu.PrefetchScalarGridSpec(
        num_scalar_prefetch=1,
        grid=(b, DIFF_HEADS, s // tq),
        in_specs=[pl.BlockSpec((1, tq, w), lambda bi, h, i, sl: (bi, i, h)),
                  pl.BlockSpec((1, s, w), lambda bi, h, i, sl: (bi, 0, h)),
                  pl.BlockSpec((1, ntk, w, tk), lambda bi, h, i, sl: (bi, 0, h, 0)),
                  _const_spec(lq1.shape), _const_spec(lk1.shape),
                  _const_spec(lq2.shape), _const_spec(lk2.shape), _const_spec(g.shape)],
        out_specs=pl.BlockSpec((1, w, tq), lambda bi, h, i, sl: (bi, h, i)),
        scratch_shapes=[pltpu.VMEM((2, tq, 2 * w), BF16),
                        pltpu.VMEM((tk, w), BF16),
                        pltpu.VMEM((2, tk, tq), BF16),
                        pltpu.VMEM((2, tk, tq), BF16),
                        pltpu.VMEM((2, 1, tq), F32),
                        pltpu.VMEM((2, 1, tq), F32),
                        pltpu.VMEM((2, w, tq), F32),
                        pltpu.VMEM((2, 1, tq), F32),
                        pltpu.VMEM((2, 1, tq), F32)])
    return pl.pallas_call(
        functools.partial(_diff_kernel, lambda_init=lambda_init),
        grid_spec=grid_spec,
        out_shape=jax.ShapeDtypeStruct((b, DIFF_HEADS * w, s), BF16),
        compiler_params=_params(3),
        name="diff_attn",
    )(slopes, qb, kb, vbt, lq1, lk1, lq2, lk2, g)


def _gelu_tanh_gate(x, val):
    k0 = math.sqrt(2.0 / math.pi)
    inner = x * (k0 + (k0 * 0.044715) * (x * x))
    hv = (0.5 * x) * val
    return hv + hv * jnp.tanh(inner)


def _ffn_kernel(ya_ref, ybt_ref, x_ref, wout_ref, gatt_ref, gpre_ref, wup_ref, cw_ref, wd_ref,
                gpost_ref, o_ref, x1_ref, h_ref, u_ref, tail_ref, acta_ref, actb_ref, acc_ref):
    i = pl.program_id(1)
    ts = x_ref.shape[1]
    d_ff = wd_ref.shape[0]
    fc = acta_ref.shape[-1]
    nlt = fc // LANES
    nchunk = d_ff // fc
    assert nchunk % 2 == 0

    @pl.when(i == 0)
    def _():
        tail_ref[...] = jnp.zeros_like(tail_ref)

    ka = ya_ref.shape[-1]
    for r in range(0, ts, OUTPROJ_ROWS):
        rows = slice(r, r + OUTPROJ_ROWS)
        mix = jnp.dot(ya_ref[0, rows, :], wout_ref[0:ka, :], preferred_element_type=F32)
        mix = mix + lax.dot_general(ybt_ref[0, :, rows], wout_ref[ka:, :],
                                    (((0,), (0,)), ((), ())), preferred_element_type=F32)
        x1 = x_ref[0, rows, :] + _rms(mix, gatt_ref[...])
        x1_ref[rows, :] = x1
        h_ref[rows, :] = _rms(x1, gpre_ref[...]).astype(BF16)

    def cols(c, half):
        return pl.ds(pl.multiple_of(half * d_ff + c * fc, fc), fc)

    def up_act(c, act_ref):
        hh = h_ref[...]
        outs = []
        for half in range(2):
            u = jnp.dot(hh, wup_ref[:, cols(c, half)], preferred_element_type=F32)
            for lt in range(nlt):
                slab = half * nlt + lt
                piece = u[:, lt * LANES:(lt + 1) * LANES]
                u_ref[slab, 0:HALO] = tail_ref[c * (2 * nlt) + slab]
                u_ref[slab, HALO:] = piece
                tail_ref[c * (2 * nlt) + slab] = piece[ts - HALO:]
            cp = cw_ref[:, cols(c, half)]
            out = cp[CONV_WIDTH:CONV_WIDTH + 1]
            for tap in range(CONV_WIDTH):
                start = HALO - (CONV_WIDTH - 1) + tap
                shifted = jnp.concatenate(
                    [u_ref[pl.ds(half * nlt + lt, 1, stride=2), pl.ds(start, ts), :].reshape(ts, LANES)
                     for lt in range(nlt)], axis=1)
                out = out + cp[tap:tap + 1] * shifted
            outs.append(out)
        act_ref[...] = _gelu_tanh_gate(outs[0], outs[1]).astype(BF16)

    def down(c, act_ref):
        rows = pl.ds(pl.multiple_of(c * fc, fc), fc)
        acc_ref[...] += jnp.dot(act_ref[...], wd_ref[rows, :], preferred_element_type=F32)

    acc_ref[...] = jnp.zeros_like(acc_ref)
    up_act(0, acta_ref)

    def body(t, carry):
        c = 2 * t + 1
        up_act(c, actb_ref)
        down(c - 1, acta_ref)
        up_act(c + 1, acta_ref)
        down(c, actb_ref)
        return carry

    lax.fori_loop(0, (nchunk - 2) // 2, body, 0)
    up_act(nchunk - 1, actb_ref)
    down(nchunk - 2, acta_ref)
    down(nchunk - 1, actb_ref)

    o_ref[0] = x1_ref[...] + _rms(acc_ref[...], gpost_ref[...])


def _outproj_ffn(ya, ybt, x3, w_out, gatt, gpre, w_up, cw, w_down, gpost):
    b, s, d = x3.shape
    ts = FFN_ROWS
    fc = FFN_CHUNK
    nslab = 2 * fc // LANES
    nchunk = w_down.shape[0] // fc
    consts = (w_out, gatt, gpre, w_up, cw, w_down, gpost)
    return pl.pallas_call(
        _ffn_kernel,
        grid=(b, s // ts),
        in_specs=[pl.BlockSpec((1, ts, ya.shape[-1]), lambda bi, i: (bi, i, 0)),
                  pl.BlockSpec((1, ybt.shape[1], ts), lambda bi, i: (bi, 0, i)),
                  pl.BlockSpec((1, ts, d), lambda bi, i: (bi, i, 0))]
                 + [_const_spec(c.shape) for c in consts],
        out_specs=pl.BlockSpec((1, ts, d), lambda bi, i: (bi, i, 0)),
        out_shape=jax.ShapeDtypeStruct((b, s, d), F32),
        scratch_shapes=[pltpu.VMEM((ts, d), F32),
                        pltpu.VMEM((ts, d), BF16),
                        pltpu.VMEM((nslab, HALO + ts, LANES), F32),
                        pltpu.VMEM((nchunk * nslab, HALO, LANES), F32),
                        pltpu.VMEM((ts, fc), BF16),
                        pltpu.VMEM((ts, fc), BF16),
                        pltpu.VMEM((ts, d), F32)],
        compiler_params=_params(2),
        name="outproj_ffn",
    )(ya, ybt, x3, *consts)


def kernel(x, attn_pre_g, w_in, swa_sinks, swa_out_g, diff_lq1, diff_lk1, diff_lq2, diff_lk2,
           diff_subln_g, w_out, attn_post_g, ffn_pre_g, w_up, conv_w, conv_b, w_down, ffn_post_g):
    batch, seq, d = x.shape
    depth = w_in.shape[0]
    slopes = _alibi_slopes(SWA_Q_HEADS + DIFF_HEADS)
    swa_bias = jnp.asarray(_swa_bias(slopes[:SWA_Q_HEADS]))
    diff_slopes = jnp.asarray(slopes[SWA_Q_HEADS:])
    qa_cols = _swa_q_columns()
    row = lambda v: v.reshape(1, -1).astype(F32)
    qa_w = SWA_Q_HEADS * HEAD_DIM
    kv_w = SWA_KV_HEADS * HEAD_DIM
    qb_w = DIFF_HEADS * 2 * HEAD_DIM
    va0 = qa_w + kv_w
    qb0 = va0 + kv_w
    vb0 = qb0 + 2 * qb_w

    for layer in range(depth):
        lambda_init = 0.8 - 0.6 * math.exp(-0.3 * layer)
        wl = w_in[layer]
        w_main = jnp.concatenate(
            [wl[:, qa_cols], wl[:, qa_w:va0], wl[:, qb0:vb0]], axis=1).astype(BF16)
        w_vt = jnp.concatenate([wl[:, va0:qb0], wl[:, vb0:]], axis=1).T.astype(BF16)
        qa, ka, qb, kb, vat, vbt, w_up_bf, w_down_bf, w_out_bf = _proj(
            x.reshape(batch * seq, d), row(attn_pre_g[layer]), w_main, w_vt,
            w_up[layer], w_down[layer], w_out[layer], batch, seq)
        sink_rows = jnp.repeat(swa_sinks[layer].astype(F32) * LOG2E, WINDOW).reshape(
            SWA_KV_HEADS, 1, SWA_GROUP * WINDOW)
        ya = _swa(qa.reshape(batch, seq, -1), ka.reshape(batch, seq, -1), vat,
                  swa_bias, sink_rows, row(swa_out_g[layer]))
        yb = _diff(qb.reshape(batch, seq, -1), kb.reshape(batch, seq, -1), vbt, diff_slopes,
                   row(diff_lq1[layer]), row(diff_lk1[layer]), row(diff_lq2[layer]),
                   row(diff_lk2[layer]), diff_subln_g[layer].reshape(-1, 1).astype(F32), lambda_init)
        cw = jnp.concatenate([conv_w[layer], conv_b[layer][None, :]], axis=0).astype(F32)
        cw = jnp.pad(cw, ((0, SUBLANES - cw.shape[0]), (0, 0)))
        x = _outproj_ffn(ya, yb, x, w_out_bf, row(attn_post_g[layer]), row(ffn_pre_g[layer]),
                         w_up_bf, cw, w_down_bf, row(ffn_post_g[layer]))
    return x
```
